```python
import math
import jax, jax.numpy as jnp
from jax import lax
import numpy as np

D_MODEL = 1024
BATCH = 8
SEQ = 2048
DEPTH = 2
DEC_BATCH = 128
DEC_SEQ = 8
PAST_LEN = 16384
PAGE_SIZE = 128

D_MIX = D_MODEL
GLA_DV = 64
GLA_WIDTH = 3 * D_MIX // 8
GLA_HEADS = GLA_WIDTH // GLA_DV
GLA_DK = GLA_DV // 2
GLA_QK = GLA_HEADS * GLA_DK
GLA_RANK = 16
GLA_GATE_NORM = 16.0
GLA_CHUNK = 64
CONF_CH = D_MIX // 4
CONF_K = 31
LN_EPS = 1e-5
SSM_WIDTH = D_MIX - GLA_WIDTH - CONF_CH
SSM_HEADDIM = 64
SSM_HEADS = SSM_WIDTH // SSM_HEADDIM
SSM_GROUPS = 2
SSM_HPG = SSM_HEADS // SSM_GROUPS
SSM_STATE = 64
SSM_GN = SSM_GROUPS * SSM_STATE
SSM_XBC = SSM_WIDTH + 2 * SSM_GN
SSM_CONV = 4
SSM_CHUNK = 64
D_FF = 2816
FFN_CONV = 3
EPS = 1e-6

IN_SIZES = (GLA_QK, GLA_QK, GLA_WIDTH, GLA_WIDTH, GLA_RANK,
            CONF_CH, CONF_CH,
            SSM_WIDTH, SSM_XBC, SSM_HEADS)
D_IN_PROJ = sum(IN_SIZES)

kernel_name = "hybrid_gla_conformer_ssd_convffn_step"


def _split(t, sizes):
    out, o = [], 0
    for s in sizes:
        out.append(t[..., o:o + s])
        o += s
    return out


def _rms(x, w):
    return x * lax.rsqrt(jnp.mean(x * x, axis=-1, keepdims=True) + EPS) * w


def rmsnorm(x, w):
    return _rms(x.astype(jnp.float32), w.astype(jnp.float32)).astype(x.dtype)


def layernorm(x, w, b):
    mu = jnp.mean(x, axis=-1, keepdims=True)
    xc = x - mu
    var = jnp.mean(xc * xc, axis=-1, keepdims=True)
    return xc * lax.rsqrt(var + LN_EPS) * w.astype(jnp.float32) + b.astype(jnp.float32)


def causal_dwconv(x_ext, w, b):
    w = w.astype(jnp.float32)
    y = lax.conv_general_dilated(x_ext, w[:, None, :], window_strides=(1,), padding='VALID',
                                 dimension_numbers=('NWC', 'WIO', 'NWC'),
                                 feature_group_count=w.shape[1])
    return y + b.astype(jnp.float32)


def gla_chunked(q, k, v, g, s0):
    Bn, L, H, DK = q.shape
    DV = v.shape[-1]
    C = math.gcd(L, GLA_CHUNK)
    N = L // C
    q, k, g = (t.reshape(Bn, N, C, H, DK) for t in (q, k, g))
    v = v.reshape(Bn, N, C, H, DV)
    b = jnp.cumsum(g, axis=2)
    b_last = b[:, :, -1:]
    q_in = q * jnp.exp(b)
    k_in = k * jnp.exp(-b)
    k_out = k * jnp.exp(b_last - b)
    mask = jnp.tril(jnp.ones((C, C), dtype=bool))
    att = jnp.einsum('bnihd,bnjhd->bnhij', q_in, k_in)
    att = jnp.where(mask, att, 0.0)
    o_intra = jnp.einsum('bnhij,bnjhv->bnihv', att, v)
    kv = jnp.einsum('bnjhd,bnjhv->bnhdv', k_out, v)
    decay = jnp.exp(b_last[:, :, 0])

    def step(S, inp):
        dec, kvc = inp
        return dec[..., None] * S + kvc, S

    s_fin, s_in = lax.scan(step, s0, (jnp.moveaxis(decay, 1, 0), jnp.moveaxis(kv, 1, 0)))
    s_in = jnp.moveaxis(s_in, 0, 1)
    o_inter = jnp.einsum('bnihd,bnhdv->bnihv', q_in, s_in)
    return (o_intra + o_inter).reshape(Bn, L, H, DV), s_fin


def ssd_chunked(x, dt, A, bm, cm, h0):
    Bn, L, G, R, P = x.shape
    S = bm.shape[-1]
    C = math.gcd(L, SSM_CHUNK)
    N = L // C
    x = x.reshape(Bn, N, C, G, R, P)
    dt = dt.reshape(Bn, N, C, G, R)
    bm = bm.reshape(Bn, N, C, G, S)
    cm = cm.reshape(Bn, N, C, G, S)
    acum = jnp.cumsum(dt * A, axis=2)
    xdt = x * dt[..., None]
    mask = jnp.tril(jnp.ones((C, C), dtype=bool))[:, :, None, None]
    diff = acum[:, :, :, None] - acum[:, :, None, :]
    lmat = jnp.exp(jnp.where(mask, diff, -jnp.inf))
    cb = jnp.einsum('bnigs,bnjgs->bnijg', cm, bm)
    y_intra = jnp.einsum('bnijgr,bnjgrp->bnigrp', cb[..., None] * lmat, xdt)
    a_last = acum[:, :, -1]
    st = jnp.einsum('bncgr,bncgrp,bncgs->bngrps', jnp.exp(a_last[:, :, None] - acum), xdt, bm)

    def step(h, inp):
        dec, s = inp
        return dec[..., None, None] * h + s, h

    h_fin, h_in = lax.scan(step, h0, (jnp.moveaxis(jnp.exp(a_last), 1, 0), jnp.moveaxis(st, 1, 0)))
    h_in = jnp.moveaxis(h_in, 0, 1)
    y_inter = jnp.einsum('bnigs,bnigr,bngrps->bnigrp', cm, jnp.exp(acum), h_in)
    return (y_intra + y_inter).reshape(Bn, L, G, R, P), h_fin


def trunk_layer(x, s_gla, s_conf, s_sconv, s_ssm, s_ffn,
                norm1, w_in, w_gk, b_gk, gla_norm, conf_w, conf_b, conf_ln_w, conf_ln_b,
                ssm_conv_w, ssm_conv_b, dt_bias, a_log, d_skip, ssm_norm, w_out,
                norm2, w_ffn_in, ffn_conv_w, ffn_conv_b, w_ffn_out):
    f32 = jnp.float32
    Bn, L, _ = x.shape
    xn = rmsnorm(x, norm1)
    proj = (xn @ w_in).astype(f32)
    q, k, v, g_o, g_lr, c_a, c_b, z, xbc, dt_raw = _split(proj, IN_SIZES)

    qh = q.reshape(Bn, L, GLA_HEADS, GLA_DK) * (GLA_DK ** -0.5)
    kh = k.reshape(Bn, L, GLA_HEADS, GLA_DK)
    vh = v.reshape(Bn, L, GLA_HEADS, GLA_DV)
    gk = jax.nn.log_sigmoid(g_lr @ w_gk.astype(f32) + b_gk.astype(f32)) / GLA_GATE_NORM
    o, s_gla_new = gla_chunked(qh, kh, vh, gk.reshape(Bn, L, GLA_HEADS, GLA_DK), s_gla.astype(f32))
    o = _rms(o, gla_norm.astype(f32)) * jax.nn.silu(g_o.reshape(Bn, L, GLA_HEADS, GLA_DV))
    o = o.reshape(Bn, L, GLA_WIDTH)

    u = c_a * jax.nn.sigmoid(c_b)
    u_ext = jnp.concatenate([s_conf.astype(f32), u], axis=1)
    c = jax.nn.silu(layernorm(causal_dwconv(u_ext, conf_w, conf_b), conf_ln_w, conf_ln_b))

    xbc_ext = jnp.concatenate([s_sconv.astype(f32), xbc], axis=1)
    xbc_c = jax.nn.silu(causal_dwconv(xbc_ext, ssm_conv_w, ssm_conv_b))
    xs, bm, cm = _split(xbc_c, (SSM_WIDTH, SSM_GN, SSM_GN))
    xs5 = xs.reshape(Bn, L, SSM_GROUPS, SSM_HPG, SSM_HEADDIM)
    dt = jax.nn.softplus(dt_raw + dt_bias.astype(f32)).reshape(Bn, L, SSM_GROUPS, SSM_HPG)
    A = -jnp.exp(a_log.astype(f32)).reshape(SSM_GROUPS, SSM_HPG)
    h0 = s_ssm.astype(f32).reshape(Bn, SSM_GROUPS, SSM_HPG, SSM_HEADDIM, SSM_STATE)
    y, h_new = ssd_chunked(xs5, dt, A, bm.reshape(Bn, L, SSM_GROUPS, SSM_STATE),
                           cm.reshape(Bn, L, SSM_GROUPS, SSM_STATE), h0)
    y = y + d_skip.astype(f32).reshape(SSM_GROUPS, SSM_HPG)[:, :, None] * xs5
    gw = SSM_HPG * SSM_HEADDIM
    y = y.reshape(Bn, L, SSM_GROUPS, gw) * jax.nn.silu(z.reshape(Bn, L, SSM_GROUPS, gw))
    y = _rms(y, ssm_norm.astype(f32).reshape(SSM_GROUPS, gw)).reshape(Bn, L, SSM_WIDTH)

    mixed = jnp.concatenate([o, c, y], axis=-1) @ w_out.astype(f32)
    x = x + mixed.astype(x.dtype)

    xn2 = rmsnorm(x, norm2)
    hf = (xn2 @ w_ffn_in).astype(f32)
    a, bg = hf[..., :D_FF], hf[..., D_FF:]
    a_ext = jnp.concatenate([s_ffn.astype(f32), a], axis=1)
    a_c = causal_dwconv(a_ext, ffn_conv_w, ffn_conv_b)
    ff = (jax.nn.silu(a_c) * bg).astype(x.dtype) @ w_ffn_out
    x = x + ff.astype(x.dtype)

    new_states = (s_gla_new.astype(s_gla.dtype),
                  u_ext[:, -(CONF_K - 1):].astype(s_conf.dtype),
                  xbc_ext[:, -(SSM_CONV - 1):].astype(s_sconv.dtype),
                  h_new.reshape(Bn, SSM_HEADS, SSM_HEADDIM, SSM_STATE).astype(s_ssm.dtype),
                  a_ext[:, -(FFN_CONV - 1):].astype(s_ffn.dtype))
    return x, new_states


def _zero_states(batch, dtype):
    return (jnp.zeros((batch, GLA_HEADS, GLA_DK, GLA_DV), dtype),
            jnp.zeros((batch, CONF_K - 1, CONF_CH), dtype),
            jnp.zeros((batch, SSM_CONV - 1, SSM_XBC), dtype),
            jnp.zeros((batch, SSM_HEADS, SSM_HEADDIM, SSM_STATE), dtype),
            jnp.zeros((batch, FFN_CONV - 1, D_FF), dtype))


def setup_inputs(seed: int = 0) -> dict:
    key = jax.random.key(seed)
    ks = jax.random.split(key, 32)
    nrm = lambda i, shape, s: jax.random.normal(ks[i], shape, jnp.float32) * s
    gain = lambda i, shape: 1.0 + 0.02 * jax.random.normal(ks[i], shape, jnp.float32)
    dt0 = jnp.exp(jax.random.uniform(ks[14], (DEPTH, SSM_HEADS), jnp.float32,
                                     math.log(1e-3), math.log(1e-1)))
    return {
        "x_prompt": nrm(0, (BATCH, SEQ, D_MODEL), 1.0),
        "x_sample": nrm(1, (DEC_BATCH, DEC_SEQ, D_MODEL), 1.0),
        "state_gla": nrm(2, (DEPTH, DEC_BATCH, GLA_HEADS, GLA_DK, GLA_DV), 0.5),
        "state_conf_conv": nrm(3, (DEPTH, DEC_BATCH, CONF_K - 1, CONF_CH), 0.5),
        "state_ssm_conv": nrm(4, (DEPTH, DEC_BATCH, SSM_CONV - 1, SSM_XBC), 1.0),
        "state_ssm": nrm(5, (DEPTH, DEC_BATCH, SSM_HEADS, SSM_HEADDIM, SSM_STATE), 0.5),
        "state_ffn_conv": nrm(6, (DEPTH, DEC_BATCH, FFN_CONV - 1, D_FF), 1.0),
        "norm1": gain(7, (DEPTH, D_MODEL)),
        "w_in": nrm(8, (DEPTH, D_MODEL, D_IN_PROJ), D_MODEL ** -0.5),
        "w_gk": nrm(9, (DEPTH, GLA_RANK, GLA_QK), GLA_RANK ** -0.5),
        "b_gk": nrm(10, (DEPTH, GLA_QK), 0.02),
        "gla_norm": gain(11, (DEPTH, GLA_DV)),
        "conf_w": nrm(12, (DEPTH, CONF_K, CONF_CH), CONF_K ** -0.5),
        "conf_b": nrm(13, (DEPTH, CONF_CH), 0.02),
        "conf_ln_w": gain(15, (DEPTH, CONF_CH)),
        "conf_ln_b": nrm(16, (DEPTH, CONF_CH), 0.02),
        "ssm_conv_w": nrm(17, (DEPTH, SSM_CONV, SSM_XBC), SSM_CONV ** -0.5),
        "ssm_conv_b": nrm(18, (DEPTH, SSM_XBC), 0.02),
        "dt_bias": dt0 + jnp.log(-jnp.expm1(-dt0)),
        "a_log": jnp.log(jax.random.uniform(ks[19], (DEPTH, SSM_HEADS), jnp.float32, 1.0, 16.0)),
        "d_skip": gain(20, (DEPTH, SSM_HEADS)),
        "ssm_norm": gain(21, (DEPTH, SSM_WIDTH)),
        "w_out": nrm(22, (DEPTH, D_MIX, D_MODEL), D_MIX ** -0.5),
        "norm2": gain(23, (DEPTH, D_MODEL)),
        "w_ffn_in": nrm(24, (DEPTH, D_MODEL, 2 * D_FF), D_MODEL ** -0.5),
        "ffn_conv_w": nrm(25, (DEPTH, FFN_CONV, D_FF), FFN_CONV ** -0.5),
        "ffn_conv_b": nrm(26, (DEPTH, D_FF), 0.02),
        "w_ffn_out": nrm(27, (DEPTH, D_FF, D_MODEL), D_FF ** -0.5),
        "final_norm": gain(28, (D_MODEL,)),
    }


def reference(x_prompt, x_sample, state_gla, state_conf_conv, state_ssm_conv, state_ssm, state_ffn_conv,
              norm1, w_in, w_gk, b_gk, gla_norm, conf_w, conf_b, conf_ln_w, conf_ln_b,
              ssm_conv_w, ssm_conv_b, dt_bias, a_log, d_skip, ssm_norm, w_out,
              norm2, w_ffn_in, ffn_conv_w, ffn_conv_b, w_ffn_out, final_norm):
    weights = (norm1, w_in, w_gk, b_gk, gla_norm, conf_w, conf_b, conf_ln_w, conf_ln_b,
               ssm_conv_w, ssm_conv_b, dt_bias, a_log, d_skip, ssm_norm, w_out,
               norm2, w_ffn_in, ffn_conv_w, ffn_conv_b, w_ffn_out)
    sample_states = (state_gla, state_conf_conv, state_ssm_conv, state_ssm, state_ffn_conv)
    zero_p = _zero_states(x_prompt.shape[0], x_prompt.dtype)
    xp, xs = x_prompt, x_sample
    new_p = [[] for _ in range(5)]
    new_s = [[] for _ in range(5)]
    for l in range(DEPTH):
        wl = [w[l] for w in weights]
        xp, sp = trunk_layer(xp, *zero_p, *wl)
        xs, ss = trunk_layer(xs, *[s[l] for s in sample_states], *wl)
        for i in range(5):
            new_p[i].append(sp[i])
            new_s[i].append(ss[i])
    y_prompt = rmsnorm(xp, final_norm)
    y_sample = rmsnorm(xs, final_norm)
    gla_p, conf_p, sconv_p, ssm_p, ffn_p = (jnp.stack(t) for t in new_p)
    gla_s, conf_s, sconv_s, ssm_s, ffn_s = (jnp.stack(t) for t in new_s)
    return (y_prompt, y_sample, gla_p, conf_p, sconv_p, ssm_p, ffn_p, gla_s, conf_s, sconv_s, ssm_s, ffn_s)
```

```python
import functools
import math

import numpy as np
import jax
import jax.numpy as jnp
from jax import lax
from jax.experimental import pallas as pl
from jax.experimental.pallas import tpu as pltpu

F32 = jnp.float32
BF16 = jnp.bfloat16

D_MODEL = 1024
DEPTH = 2
GLA_HEADS = 6
GLA_DK = 32
GLA_DV = 64
GLA_QK = GLA_HEADS * GLA_DK
GLA_WIDTH = GLA_HEADS * GLA_DV
GLA_RANK = 16
GLA_GATE_NORM = 16.0
GLA_CHUNK = 64
CONF_CH = 256
CONF_K = 31
LN_EPS = 1e-5
SSM_WIDTH = 384
SSM_HEADDIM = 64
SSM_HEADS = 6
SSM_GROUPS = 2
SSM_HPG = 3
SSM_STATE = 64
SSM_GN = SSM_GROUPS * SSM_STATE
SSM_XBC = SSM_WIDTH + 2 * SSM_GN
SSM_CONV = 4
SSM_CHUNK = 64
D_FF = 2816
FFN_CONV = 3
EPS = 1e-6
IN_SIZES = (GLA_QK, GLA_QK, GLA_WIDTH, GLA_WIDTH, GLA_RANK, CONF_CH, CONF_CH, SSM_WIDTH, SSM_XBC, SSM_HEADS)

LANE = 128
SUBLANE = 8
MXU_DIM = 256
VMEM_LIMIT_BYTES = 56 * 1024 * 1024

QK_PAD = 256
Q0 = 0
K0 = Q0 + QK_PAD
V0 = K0 + QK_PAD
GO0 = V0 + GLA_WIDTH
CA0 = GO0 + GLA_WIDTH
CB0 = CA0 + CONF_CH
Z0 = CB0 + CONF_CH
XBC0 = Z0 + SSM_WIDTH
DT0 = XBC0 + SSM_XBC
GLR0 = DT0 + SSM_WIDTH
NP = GLR0 + LANE

FFN_SPLITS = (768, 768, 768, 512)

CONF_BASE = 32
SSMC_BASE = 8
FFNC_BASE = 8


def _dot(a, b):
    return jnp.dot(a, b, preferred_element_type=F32)


def _dot_nt(a, b):
    return lax.dot_general(a, b, (((1,), (1,)), ((), ())), preferred_element_type=F32)


def _dot_tn(a, b):
    return lax.dot_general(a, b, (((0,), (0,)), ((), ())), preferred_element_type=F32)


def _split_bf16(a, parts):
    out = []
    r = a
    for i in range(parts):
        p = r.astype(BF16)
        out.append(p)
        if i + 1 < parts:
            r = r - p.astype(F32)
    return out


def _sigmoid(x):
    return 1.0 / (1.0 + jnp.exp(-x))


def _silu(x):
    return x * _sigmoid(x)


def _softplus(x):
    return jnp.maximum(x, 0.0) + jnp.log1p(jnp.exp(-jnp.abs(x)))


def _log_sigmoid(x):
    return jnp.minimum(x, 0.0) - jnp.log1p(jnp.exp(-jnp.abs(x)))


def _pick_diag_blocks(stacked, c, lane_iota):
    res = stacked[5 * c:6 * c, :]
    for h in range(4, -1, -1):
        res = jnp.where(lane_iota < (h + 1) * 64, stacked[h * c:(h + 1) * c, :], res)
    return res


def _mixer_kernel(C, T, bb, nl,
                  x_ref, sgla_ref, sconf_ref, ssc_ref, sssm_ref,
                  n1_ref, win_ref, wgk_ref, bgk_ref, gnorm_ref, cw_ref, cbias_ref, lnw_ref, lnb_ref,
                  scw_ref, scb_ref, dtb_ref, alog_ref, dskip_ref, snorm_ref, wout_ref,
                  tril_ref, tril6_ref, hm6_ref, e6_ref, gm6_ref, stmask_ref, htmask_ref, bhead_ref, bgroup_ref,
                  xo_ref, ogla_ref, oconf_ref, osc_ref, ossm_ref,
                  proj_scr, st_scr, ht_scr, ubuf, xbuf, mix_scr):
    l = pl.program_id(1)
    nchunk = T // C

    x = x_ref[...]
    xn = x * lax.rsqrt(jnp.mean(x * x, axis=-1, keepdims=True) + EPS) * n1_ref[...]
    proj_scr[...] = _dot(xn.astype(BF16), win_ref[...])

    lane384 = lax.broadcasted_iota(jnp.int32, (C, GLA_WIDTH), 1)
    tril = tril_ref[...]
    tril6 = tril6_ref[...] > 0.5

    def cumsum_rows(a):
        acc = None
        for p in _split_bf16(a, 3):
            t = _dot(tril, p)
            acc = t if acc is None else acc + t
        return acc

    def seg_mean(a, ones_ref, inv_n):
        acc = None
        for p in _split_bf16(a, 2):
            t = _dot(p, ones_ref[...])
            acc = t if acc is None else acc + t
        return acc * inv_n

    def chunk(r0, c0):
        def P(off, width):
            return proj_scr[pl.ds(r0, C), off:off + width]

        gk_pre = _dot(P(GLR0, LANE).astype(BF16), wgk_ref[...]) + bgk_ref[...]
        gk = _log_sigmoid(gk_pre) * (1.0 / GLA_GATE_NORM)
        b = cumsum_rows(gk)
        blast = b[C - 1:C, :]
        q_in = P(Q0, QK_PAD) * (GLA_DK ** -0.5) * jnp.exp(b)
        k = P(K0, QK_PAD)
        k_in = k * jnp.exp(-b)
        k_out = k * jnp.exp(blast - b)
        vb = P(V0, GLA_WIDTH).astype(BF16)
        qs = (jnp.concatenate([q_in] * GLA_HEADS, axis=0) * hm6_ref[...]).astype(BF16)
        att = _dot_nt(qs, k_in.astype(BF16))
        att = jnp.where(tril6, att, 0.0)
        o_intra = _pick_diag_blocks(_dot(att.astype(BF16), vb), C, lane384)
        st = st_scr[...]
        q_inb = q_in.astype(BF16)
        o = o_intra + _dot_nt(q_inb, st.astype(BF16))
        kv = _dot_tn(vb, k_out.astype(BF16))
        st_scr[...] = st * jnp.exp(blast) + kv * stmask_ref[...]
        ms = seg_mean(o * o, bhead_ref, 1.0 / GLA_DV)
        o = o * lax.rsqrt(ms + EPS) * gnorm_ref[...] * _silu(P(GO0, GLA_WIDTH))
        mix_scr[pl.ds(r0, C), 0:GLA_WIDTH] = o.astype(mix_scr.dtype)

        u = P(CA0, CONF_CH) * _sigmoid(P(CB0, CONF_CH))
        b0 = pl.multiple_of(CONF_BASE + c0, SUBLANE)
        ext = jnp.concatenate([ubuf[0, pl.ds(b0 - SUBLANE, SUBLANE), :], u], axis=0)
        ubuf[0, pl.ds(b0, C), :] = u
        for s in range(1, SUBLANE):
            ubuf[s, pl.ds(b0 - SUBLANE, C), :] = ext[s:s + C, :]
        acc = jnp.broadcast_to(cbias_ref[...], (C, CONF_CH))
        for kk in range(CONF_K):
            s = (kk - (CONF_K - 1)) % SUBLANE
            off = kk - (CONF_K - 1) - s
            acc = acc + cw_ref[kk:kk + 1, :] * ubuf[s, pl.ds(b0 + off, C), :]
        mu = jnp.mean(acc, axis=-1, keepdims=True)
        xc = acc - mu
        var = jnp.mean(xc * xc, axis=-1, keepdims=True)
        cval = _silu(xc * lax.rsqrt(var + LN_EPS) * lnw_ref[...] + lnb_ref[...])
        mix_scr[pl.ds(r0, C), GLA_WIDTH:GLA_WIDTH + CONF_CH] = cval.astype(mix_scr.dtype)

        xbc = P(XBC0, SSM_XBC)
        ext = jnp.concatenate([xbuf[pl.ds(c0, SUBLANE), :], xbc], axis=0)
        xbuf[pl.ds(SSMC_BASE + c0, C), :] = xbc
        acc = jnp.broadcast_to(scb_ref[...], (C, SSM_XBC))
        for kk in range(SSM_CONV):
            lo = SUBLANE - (SSM_CONV - 1) + kk
            acc = acc + scw_ref[kk:kk + 1, :] * ext[lo:lo + C, :]
        xcv = _silu(acc)
        xs = xcv[:, 0:SSM_WIDTH]
        bm = xcv[:, SSM_WIDTH:SSM_WIDTH + SSM_GN]
        cm = xcv[:, SSM_WIDTH + SSM_GN:SSM_XBC]
        dtf = _softplus(P(DT0, SSM_WIDTH) + dtb_ref[...])
        af = dtf * (-jnp.exp(alog_ref[...]))
        acum = cumsum_rows(af)
        alast = acum[C - 1:C, :]
        xdt = xs * dtf
        xdtb = xdt.astype(BF16)
        row_all = None
        for p in _split_bf16(acum, 3):
            t = _dot_nt(e6_ref[...], p)
            row_all = t if row_all is None else row_all + t
        col_all = jnp.concatenate([acum[:, h * 64:h * 64 + C] for h in range(SSM_HEADS)], axis=0)
        lmat = jnp.exp(jnp.where(tril6, col_all - row_all, -jnp.inf))
        cms = (jnp.concatenate([cm] * SSM_HEADS, axis=0) * gm6_ref[...]).astype(BF16)
        cb_all = _dot_nt(cms, bm.astype(BF16))
        y_intra = _pick_diag_blocks(_dot((cb_all * lmat).astype(BF16), xdtb), C, lane384)
        ht = ht_scr[...]
        y_inter = _dot(cm.astype(BF16), ht.astype(BF16)) * jnp.exp(acum)
        wx = xdt * jnp.exp(alast - acum)
        stn = _dot_tn(bm.astype(BF16), wx.astype(BF16))
        ht_scr[...] = ht * jnp.exp(alast) + stn * htmask_ref[...]
        y = y_intra + y_inter + dskip_ref[...] * xs
        y = y * _silu(P(Z0, SSM_WIDTH))
        ms = seg_mean(y * y, bgroup_ref, 1.0 / (SSM_HPG * SSM_HEADDIM))
        y = y * lax.rsqrt(ms + EPS) * snorm_ref[...]
        mix_scr[pl.ds(r0, C), GLA_WIDTH + CONF_CH:D_MODEL] = y.astype(mix_scr.dtype)

    def load_conv_hist(hconf, hsc):
        hist = jnp.concatenate([jnp.zeros((CONF_BASE - (CONF_K - 1), CONF_CH), F32), hconf], axis=0)
        ubuf[0, 0:CONF_BASE, :] = hist
        for s in range(1, SUBLANE):
            ubuf[s, 0:CONF_BASE - SUBLANE, :] = hist[s:s + CONF_BASE - SUBLANE, :]
        xbuf[0:SSMC_BASE, :] = jnp.concatenate(
            [jnp.zeros((SSMC_BASE - (SSM_CONV - 1), SSM_XBC), F32), hsc], axis=0)

    def load_state(j, first):
        if first:
            s = sgla_ref[j]
            rows = []
            for h in range(GLA_HEADS):
                parts = []
                if h > 0:
                    parts.append(jnp.zeros((GLA_DK, h * GLA_DV), F32))
                parts.append(s[h * GLA_DK:(h + 1) * GLA_DK, :])
                if h < GLA_HEADS - 1:
                    parts.append(jnp.zeros((GLA_DK, (GLA_HEADS - 1 - h) * GLA_DV), F32))
                rows.append(jnp.concatenate(parts, axis=1))
            rows.append(jnp.zeros((QK_PAD - GLA_QK, GLA_WIDTH), F32))
            st_scr[...] = jnp.concatenate(rows, axis=0).T
            hs = sssm_ref[j]
            rows = []
            for h in range(SSM_HEADS):
                blk = hs[h * SSM_HEADDIM:(h + 1) * SSM_HEADDIM, :]
                z = jnp.zeros((SSM_HEADDIM, SSM_STATE), F32)
                rows.append(jnp.concatenate([blk, z] if h // SSM_HPG == 0 else [z, blk], axis=1))
            ht_scr[...] = jnp.concatenate(rows, axis=0).T
            load_conv_hist(sconf_ref[j], ssc_ref[j])
        else:
            load_conv_hist(oconf_ref[j], osc_ref[j])

    def store_state(j):
        stt = st_scr[...].T
        for h in range(GLA_HEADS):
            ogla_ref[j, h * GLA_DK:(h + 1) * GLA_DK, :] = stt[h * GLA_DK:(h + 1) * GLA_DK, h * GLA_DV:(h + 1) * GLA_DV]
        htt = ht_scr[...].T
        for h in range(SSM_HEADS):
            g = h // SSM_HPG
            ossm_ref[j, h * SSM_HEADDIM:(h + 1) * SSM_HEADDIM, :] = (
                htt[h * SSM_HEADDIM:(h + 1) * SSM_HEADDIM, g * SSM_STATE:(g + 1) * SSM_STATE])

    def seq_body(j):
        if nl == 1:
            load_state(j, True)
        else:
            @pl.when(l == 0)
            def _():
                load_state(j, True)

            @pl.when(l > 0)
            def _():
                load_state(j, False)

        if nchunk == 1:
            chunk(pl.multiple_of(j * T, SUBLANE), 0)
        else:
            def chunk_body(ci, carry):
                c0 = pl.multiple_of(ci * C, SUBLANE)
                chunk(pl.multiple_of(j * T + c0, SUBLANE), c0)
                return carry
            lax.fori_loop(0, nchunk, chunk_body, 0)

        oconf_ref[j] = ubuf[0, T + CONF_BASE - (CONF_K - 1):T + CONF_BASE, :]
        osc_ref[j] = xbuf[T + SSMC_BASE - (SSM_CONV - 1):T + SSMC_BASE, :]
        if nl == 1:
            store_state(j)
        else:
            @pl.when(l == nl - 1)
            def _():
                store_state(j)

    if bb == 1:
        seq_body(0)
    else:
        def seq_loop(j, carry):
            seq_body(j)
            return carry
        lax.fori_loop(0, bb, seq_loop, 0)

    xo_ref[...] = x + _dot(mix_scr[...].astype(BF16), wout_ref[...])


def _np_consts(C):
    i = np.arange(C)
    tril = (i[None, :] <= i[:, None]).astype(np.float32)
    tril6 = np.tile(tril, (6, 1))
    hm6 = np.zeros((6 * C, QK_PAD), np.float32)
    e6 = np.zeros((6 * C, GLA_WIDTH), np.float32)
    gm6 = np.zeros((6 * C, SSM_GN), np.float32)
    for h in range(6):
        hm6[h * C:(h + 1) * C, h * GLA_DK:(h + 1) * GLA_DK] = 1.0
        e6[h * C:(h + 1) * C, h * 64] = 1.0
        g = h // SSM_HPG
        gm6[h * C:(h + 1) * C, g * SSM_STATE:(g + 1) * SSM_STATE] = 1.0
    stmask = np.zeros((GLA_WIDTH, QK_PAD), np.float32)
    htmask = np.zeros((SSM_GN, SSM_WIDTH), np.float32)
    bhead = np.zeros((GLA_WIDTH, GLA_WIDTH), np.float32)
    bgroup = np.zeros((SSM_WIDTH, SSM_WIDTH), np.float32)
    for h in range(6):
        stmask[h * 64:(h + 1) * 64, h * 32:(h + 1) * 32] = 1.0
        g = h // SSM_HPG
        htmask[g * 64:(g + 1) * 64, h * 64:(h + 1) * 64] = 1.0
        bhead[h * 64:(h + 1) * 64, h * 64:(h + 1) * 64] = 1.0
    gw = SSM_HPG * SSM_HEADDIM
    for g in range(SSM_GROUPS):
        bgroup[g * gw:(g + 1) * gw, g * gw:(g + 1) * gw] = 1.0
    return (jnp.asarray(tril, BF16), jnp.asarray(tril6, F32), jnp.asarray(hm6, F32), jnp.asarray(e6, BF16),
            jnp.asarray(gm6, F32), jnp.asarray(stmask, F32), jnp.asarray(htmask, F32),
            jnp.asarray(bhead, BF16), jnp.asarray(bgroup, BF16))


def _full_spec(a):
    nd = a.ndim
    return pl.BlockSpec(a.shape, lambda i, l: (0,) * nd)


def _mixer_call(x2d, states, wts, Bn, L, bb, T, C):
    assert L % T == 0 and T % C == 0 and Bn % bb == 0 and C % SUBLANE == 0
    nl = L // T
    assert bb == 1 or nl == 1
    nb = Bn // bb
    R = bb * T
    consts = _np_consts(C)
    sgla, sconf, ssc, sssm = states
    x_spec = pl.BlockSpec((R, D_MODEL), lambda i, l: (i * nl + l, 0))

    def st_spec(a):
        return pl.BlockSpec((bb,) + a.shape[1:], lambda i, l: (i, 0, 0))

    in_specs = ([x_spec] + [st_spec(a) for a in states] + [_full_spec(a) for a in wts]
                + [_full_spec(a) for a in consts])
    out_shape = ([jax.ShapeDtypeStruct((Bn * L, D_MODEL), F32)]
                 + [jax.ShapeDtypeStruct(a.shape, F32) for a in states])
    out_specs = [x_spec] + [st_spec(a) for a in states]
    scratch = [
        pltpu.VMEM((R, NP), F32),
        pltpu.VMEM((GLA_WIDTH, QK_PAD), F32),
        pltpu.VMEM((SSM_GN, SSM_WIDTH), F32),
        pltpu.VMEM((SUBLANE, CONF_BASE + T, CONF_CH), F32),
        pltpu.VMEM((SSMC_BASE + T, SSM_XBC), F32),
        pltpu.VMEM((R, D_MODEL), BF16 if C % (2 * SUBLANE) == 0 else F32),
    ]
    return pl.pallas_call(
        functools.partial(_mixer_kernel, C, T, bb, nl),
        grid=(nb, nl),
        in_specs=in_specs,
        out_specs=out_specs,
        out_shape=out_shape,
        scratch_shapes=scratch,
        compiler_params=pltpu.CompilerParams(
            dimension_semantics=("arbitrary", "arbitrary"), vmem_limit_bytes=VMEM_LIMIT_BYTES),
        name="mixer",
    )(x2d, sgla, sconf, ssc, sssm, *wts, *consts)


def _ffn_kernel(T, bb, nl, final,
                x_ref, sffn_ref, n2_ref, wa_ref, wg_ref, fcw_ref, fcb_ref, wo_ref, fn_ref,
                xo_ref, offn_ref, abuf):
    l = pl.program_id(1)
    R = bb * T
    x = x_ref[...]
    xn = (x * lax.rsqrt(jnp.mean(x * x, axis=-1, keepdims=True) + EPS) * n2_ref[...]).astype(BF16)
    f0 = 0
    for fi, fw in enumerate(FFN_SPLITS):
        a = _dot(xn, wa_ref[:, f0:f0 + fw])
        g = _dot(xn, wg_ref[:, f0:f0 + fw])
        abuf[:, FFNC_BASE:FFNC_BASE + T, 0:fw] = a.reshape(bb, T, fw)
        lo = FFNC_BASE - (FFN_CONV - 1)
        if nl == 1:
            abuf[:, lo:FFNC_BASE, 0:fw] = sffn_ref[:, :, f0:f0 + fw]
        else:
            @pl.when(l == 0)
            def _():
                abuf[:, lo:FFNC_BASE, 0:fw] = sffn_ref[:, :, f0:f0 + fw]

            @pl.when(l > 0)
            def _():
                abuf[:, lo:FFNC_BASE, 0:fw] = offn_ref[:, :, f0:f0 + fw]
        ac = fcb_ref[:, f0:f0 + fw][None]
        for kk in range(FFN_CONV):
            ac = ac + fcw_ref[kk:kk + 1, f0:f0 + fw][None] * abuf[:, lo + kk:lo + kk + T, 0:fw]
        offn_ref[:, :, f0:f0 + fw] = abuf[:, T + lo:T + FFNC_BASE, 0:fw]
        act = (_silu(ac).reshape(R, fw) * g).astype(BF16)
        part = _dot(act, wo_ref[f0:f0 + fw, :])
        if fi == 0:
            xo_ref[...] = x + part
        else:
            xo_ref[...] += part
        f0 += fw
    if final:
        y = xo_ref[...]
        xo_ref[...] = y * lax.rsqrt(jnp.mean(y * y, axis=-1, keepdims=True) + EPS) * fn_ref[...]


def _ffn_call(x2d, sffn, wts, Bn, L, bb, T, final):
    assert L % T == 0 and Bn % bb == 0
    nl = L // T
    assert bb == 1 or nl == 1
    nb = Bn // bb
    R = bb * T
    x_spec = pl.BlockSpec((R, D_MODEL), lambda i, l: (i * nl + l, 0))
    s_spec = pl.BlockSpec((bb, FFN_CONV - 1, D_FF), lambda i, l: (i, 0, 0))
    return pl.pallas_call(
        functools.partial(_ffn_kernel, T, bb, nl, final),
        grid=(nb, nl),
        in_specs=[x_spec, s_spec] + [_full_spec(a) for a in wts],
        out_specs=[x_spec, s_spec],
        out_shape=[jax.ShapeDtypeStruct((Bn * L, D_MODEL), F32),
                   jax.ShapeDtypeStruct((Bn, FFN_CONV - 1, D_FF), F32)],
        scratch_shapes=[pltpu.VMEM((bb, FFNC_BASE + T, max(FFN_SPLITS)), F32)],
        compiler_params=pltpu.CompilerParams(
            dimension_semantics=("arbitrary", "arbitrary"), vmem_limit_bytes=VMEM_LIMIT_BYTES),
        name="ffn",
    )(x2d, sffn, *wts)


def _prep_layer(l, norm1, w_in, w_gk, b_gk, gla_norm, conf_w, conf_b, conf_ln_w, conf_ln_b,
                ssm_conv_w, ssm_conv_b, dt_bias, a_log, d_skip, ssm_norm, w_out,
                norm2, w_ffn_in, ffn_conv_w, ffn_conv_b, w_ffn_out):
    offs = np.concatenate([[0], np.cumsum(IN_SIZES)])
    w = w_in[l]
    seg = lambda i: w[:, int(offs[i]):int(offs[i + 1])]
    zpad = lambda n: jnp.zeros((D_MODEL, n), w.dtype)
    w_in_p = jnp.concatenate([
        seg(0), zpad(QK_PAD - GLA_QK), seg(1), zpad(QK_PAD - GLA_QK), seg(2), seg(3),
        seg(5), seg(6), seg(7), seg(8), jnp.repeat(seg(9), SSM_HEADDIM, axis=1),
        seg(4), zpad(LANE - GLA_RANK)], axis=1).astype(BF16)
    assert w_in_p.shape == (D_MODEL, NP)
    row = lambda v: v.reshape(1, -1).astype(F32)
    wgk_p = jnp.zeros((LANE, QK_PAD), F32).at[:GLA_RANK, :GLA_QK].set(w_gk[l]).astype(BF16)
    bgk_p = jnp.zeros((1, QK_PAD), F32).at[0, :GLA_QK].set(b_gk[l])
    rep64 = lambda v: row(jnp.repeat(v, SSM_HEADDIM))
    mixer_w = (row(norm1[l]), w_in_p, wgk_p, bgk_p, row(jnp.tile(gla_norm[l], GLA_HEADS)),
               conf_w[l].astype(F32), row(conf_b[l]), row(conf_ln_w[l]), row(conf_ln_b[l]),
               ssm_conv_w[l].astype(F32), row(ssm_conv_b[l]), rep64(dt_bias[l]), rep64(a_log[l]),
               rep64(d_skip[l]), row(ssm_norm[l]), w_out[l].astype(BF16))
    ffn_w = (row(norm2[l]), w_ffn_in[l][:, :D_FF].astype(BF16), w_ffn_in[l][:, D_FF:].astype(BF16),
             ffn_conv_w[l].astype(F32), row(ffn_conv_b[l]), w_ffn_out[l].astype(BF16))
    return mixer_w, ffn_w


def _run_group(x, states, layer_w, final_norm, bb, T, C):
    Bn, L, _ = x.shape
    x2d = x.reshape(Bn * L, D_MODEL)
    fn = final_norm.reshape(1, -1).astype(F32)
    new_states = []
    for l in range(DEPTH):
        mixer_w, ffn_w = layer_w[l]
        sgla, sconf, ssc, sssm, sffn = states[l]
        x2d, ngla, nconf, nsc, nssm = _mixer_call(x2d, (sgla, sconf, ssc, sssm), mixer_w, Bn, L, bb, T, C)
        x2d, nffn = _ffn_call(x2d, sffn, ffn_w + (fn,), Bn, L, bb, T, final=(l == DEPTH - 1))
        new_states.append((ngla, nconf, nsc, nssm, nffn))
    return x2d.reshape(Bn, L, D_MODEL), new_states


def kernel(x_prompt, x_sample, state_gla, state_conf_conv, state_ssm_conv, state_ssm, state_ffn_conv, norm1, w_in, w_gk, b_gk, gla_norm, conf_w, conf_b, conf_ln_w, conf_ln_b, ssm_conv_w, ssm_conv_b, dt_bias, a_log, d_skip, ssm_norm, w_out, norm2, w_ffn_in, ffn_conv_w, ffn_conv_b, w_ffn_out, final_norm):
    weights = (norm1, w_in, w_gk, b_gk, gla_norm, conf_w, conf_b, conf_ln_w, conf_ln_b,
               ssm_conv_w, ssm_conv_b, dt_bias, a_log, d_skip, ssm_norm, w_out,
               norm2, w_ffn_in, ffn_conv_w, ffn_conv_b, w_ffn_out)
    layer_w = [_prep_layer(l, *weights) for l in range(DEPTH)]
    Bp, Lp, _ = x_prompt.shape
    Bs, Ls, _ = x_sample.shape

    def zero_states(bn):
        return (jnp.zeros((bn, GLA_QK, GLA_DV), F32), jnp.zeros((bn, CONF_K - 1, CONF_CH), F32),
                jnp.zeros((bn, SSM_CONV - 1, SSM_XBC), F32), jnp.zeros((bn, SSM_WIDTH, SSM_STATE), F32),
                jnp.zeros((bn, FFN_CONV - 1, D_FF), F32))

    p_states = [zero_states(Bp) for _ in range(DEPTH)]
    s_states = [(state_gla[l].reshape(Bs, GLA_QK, GLA_DV), state_conf_conv[l], state_ssm_conv[l],
                 state_ssm[l].reshape(Bs, SSM_WIDTH, SSM_STATE), state_ffn_conv[l]) for l in range(DEPTH)]

    tp = min(512, Lp)
    y_p, np_states = _run_group(x_prompt, p_states, layer_w, final_norm, 1, tp, math.gcd(tp, GLA_CHUNK))
    bbs = min(16, Bs)
    y_s, ns_states = _run_group(x_sample, s_states, layer_w, final_norm, bbs, Ls, math.gcd(Ls, GLA_CHUNK))

    def pack(sts, bn):
        gla = jnp.stack([s[0].reshape(bn, GLA_HEADS, GLA_DK, GLA_DV) for s in sts])
        conf = jnp.stack([s[1] for s in sts])
        sconv = jnp.stack([s[2] for s in sts])
        ssm = jnp.stack([s[3].reshape(bn, SSM_HEADS, SSM_HEADDIM, SSM_STATE) for s in sts])
        ffn = jnp.stack([s[4] for s in sts])
        return gla, conf, sconv, ssm, ffn

    return (y_p, y_s) + pack(np_states, Bp) + pack(ns_states, Bs)
```

```python
import functools
import math

import numpy as np
import jax
import jax.numpy as jnp
from jax import lax
from jax.experimental import pallas as pl
from jax.experimental.pallas import tpu as pltpu

F32 = jnp.float32
BF16 = jnp.bfloat16

D_MODEL = 1024
DEPTH = 2
GLA_HEADS = 6
GLA_DK = 32
GLA_DV = 64
GLA_QK = GLA_HEADS * GLA_DK
GLA_WIDTH = GLA_HEADS * GLA_DV
GLA_RANK = 16
GLA_GATE_NORM = 16.0
GLA_CHUNK = 64
CONF_CH = 256
CONF_K = 31
LN_EPS = 1e-5
SSM_WIDTH = 384
SSM_HEADDIM = 64
SSM_HEADS = 6
SSM_GROUPS = 2
SSM_HPG = 3
SSM_STATE = 64
SSM_GN = SSM_GROUPS * SSM_STATE
SSM_XBC = SSM_WIDTH + 2 * SSM_GN
SSM_CONV = 4
SSM_CHUNK = 64
D_FF = 2816
FFN_CONV = 3
EPS = 1e-6
IN_SIZES = (GLA_QK, GLA_QK, GLA_WIDTH, GLA_WIDTH, GLA_RANK, CONF_CH, CONF_CH, SSM_WIDTH, SSM_XBC, SSM_HEADS)

LANE = 128
SUBLANE = 8
MXU_DIM = 256
VMEM_LIMIT_BYTES = 56 * 1024 * 1024

QK_PAD = 256
Q0 = 0
K0 = Q0 + QK_PAD
V0 = K0 + QK_PAD
GO0 = V0 + GLA_WIDTH
CA0 = GO0 + GLA_WIDTH
CB0 = CA0 + CONF_CH
Z0 = CB0 + CONF_CH
XBC0 = Z0 + SSM_WIDTH
DT0 = XBC0 + SSM_XBC
GLR0 = DT0 + SSM_WIDTH
NP = GLR0 + LANE

SPLIT_PARTS = 2

FFN_SPLITS = (768, 768, 768, 512)

CONF_BASE = 32
SSMC_BASE = 8
FFNC_BASE = 8


def _dot(a, b):
    return jnp.dot(a, b, preferred_element_type=F32)


def _dot_nt(a, b):
    return lax.dot_general(a, b, (((1,), (1,)), ((), ())), preferred_element_type=F32)


def _dot_tn(a, b):
    return lax.dot_general(a, b, (((0,), (0,)), ((), ())), preferred_element_type=F32)


def _split_bf16(a, parts):
    out = []
    r = a
    for i in range(parts):
        p = r.astype(BF16)
        out.append(p)
        if i + 1 < parts:
            r = r - p.astype(F32)
    return out


def _sigmoid(x):
    return 1.0 / (1.0 + jnp.exp(-x))


def _silu(x):
    return x * _sigmoid(x)


def _softplus(x):
    return jnp.maximum(x, 0.0) + jnp.log(1.0 + jnp.exp(-jnp.abs(x)))


def _log_sigmoid(x):
    return jnp.minimum(x, 0.0) - jnp.log(1.0 + jnp.exp(-jnp.abs(x)))


def _per_head_matmul(stacked, rhs, c):
    first_head = lax.broadcasted_iota(jnp.int32, (c, LANE), 1) < 64
    cols = []
    for col in range(3):
        two = _dot(stacked[2 * col * c:(2 * col + 2) * c, :], rhs[:, col * LANE:(col + 1) * LANE])
        cols.append(jnp.where(first_head, two[0:c, :], two[c:2 * c, :]))
    return jnp.concatenate(cols, axis=1)


def _mixer_kernel(C, T, bb, nl,
                  x_ref, sgla_ref, sconf_ref, ssc_ref, sssm_ref,
                  n1_ref, win_ref, wgk_ref, bgk_ref, gnorm_ref, cw_ref, cbias_ref, lnw_ref, lnb_ref,
                  scw_ref, scb_ref, dtb_ref, alog_ref, dskip_ref, snorm_ref, wout_ref,
                  tril_ref, tril6_ref, hm6_ref, e6_ref, gm6_ref, stmask_ref, htmask_ref, bhead_ref, bgroup_ref,
                  xo_ref, ogla_ref, oconf_ref, osc_ref, ossm_ref,
                  proj_scr, st_scr, ht_scr, ubuf, xbuf, mix_scr):
    l = pl.program_id(1)
    nchunk = T // C

    ngroups = 2 if (bb == 1 and nchunk % 2 == 0) else 1
    RG = bb * T // ngroups

    def in_proj(g):
        x = x_ref[g * RG:(g + 1) * RG, :]
        xn = x * lax.rsqrt(jnp.mean(x * x, axis=-1, keepdims=True) + EPS) * n1_ref[...]
        proj_scr[g * RG:(g + 1) * RG, :] = _dot(xn.astype(BF16), win_ref[...])

    def out_proj(g):
        rows = slice(g * RG, (g + 1) * RG)
        xo_ref[rows, :] = x_ref[rows, :] + _dot(mix_scr[rows, :].astype(BF16), wout_ref[...])

    for g in range(ngroups):
        in_proj(g)

    tril = tril_ref[...]
    tril6 = tril6_ref[...] > 0.5

    def cumsum_rows(a):
        acc = None
        for p in _split_bf16(a, SPLIT_PARTS):
            t = _dot(tril, p)
            acc = t if acc is None else acc + t
        return acc

    def seg_mean(a, ones_ref, inv_n):
        acc = None
        for p in _split_bf16(a, SPLIT_PARTS):
            t = _dot(p, ones_ref[...])
            acc = t if acc is None else acc + t
        return acc * inv_n

    def chunk(r0, c0):
        def P(off, width):
            return proj_scr[pl.ds(r0, C), off:off + width]


        gk_pre = _dot(P(GLR0, LANE).astype(BF16), wgk_ref[...]) + bgk_ref[...]
        xbc = P(XBC0, SSM_XBC)
        ext = jnp.concatenate([xbuf[pl.ds(c0, SUBLANE), :], xbc], axis=0)
        xbuf[pl.ds(SSMC_BASE + c0, C), :] = xbc
        acc = jnp.broadcast_to(scb_ref[...], (C, SSM_XBC))
        for kk in range(SSM_CONV):
            lo = SUBLANE - (SSM_CONV - 1) + kk
            acc = acc + scw_ref[kk:kk + 1, :] * ext[lo:lo + C, :]
        xcv = _silu(acc)
        xs = xcv[:, 0:SSM_WIDTH]
        bm = xcv[:, SSM_WIDTH:SSM_WIDTH + SSM_GN]
        cm = xcv[:, SSM_WIDTH + SSM_GN:SSM_XBC]
        dtf = _softplus(P(DT0, SSM_WIDTH) + dtb_ref[...])
        af = dtf * (-jnp.exp(alog_ref[...]))

        gk = _log_sigmoid(gk_pre) * (1.0 / GLA_GATE_NORM)
        acum = cumsum_rows(af)
        b = cumsum_rows(gk)
        u = P(CA0, CONF_CH) * _sigmoid(P(CB0, CONF_CH))
        b0 = CONF_BASE + c0 if isinstance(c0, int) else pl.multiple_of(CONF_BASE + c0, SUBLANE)
        uext = jnp.concatenate([ubuf[0, pl.ds(b0 - SUBLANE, SUBLANE), :], u], axis=0)
        ubuf[0, pl.ds(b0, C), :] = u
        for s in range(1, SUBLANE):
            ubuf[s, pl.ds(b0 - SUBLANE, C), :] = uext[s:s + C, :]
        xdt = xs * dtf
        xdtb = xdt.astype(BF16)
        cms = (jnp.concatenate([cm] * SSM_HEADS, axis=0) * gm6_ref[...]).astype(BF16)
        cb_all = _dot_nt(cms, bm.astype(BF16))

        def conf_taps(acc, k_lo, k_hi):
            for kk in range(k_lo, k_hi):
                s = (kk - (CONF_K - 1)) % SUBLANE
                off = kk - (CONF_K - 1) - s
                acc = acc + cw_ref[kk:kk + 1, :] * ubuf[s, pl.ds(b0 + off, C), :]
            return acc

        alast = acum[C - 1:C, :]
        if C == SSM_CHUNK:
            acum_t = acum.T
            row_all = jnp.concatenate(
                [jnp.broadcast_to(acum_t[h * 64:h * 64 + 1, :], (C, C)) for h in range(SSM_HEADS)], axis=0)
        else:
            row_all = None
            for p in _split_bf16(acum, SPLIT_PARTS):
                t = _dot_nt(e6_ref[...], p)
                row_all = t if row_all is None else row_all + t
        blast = b[C - 1:C, :]
        q_in = P(Q0, QK_PAD) * (GLA_DK ** -0.5) * jnp.exp(b)
        k = P(K0, QK_PAD)
        k_in = k * jnp.exp(-b)
        k_out = k * jnp.exp(blast - b)
        vb = P(V0, GLA_WIDTH).astype(BF16)
        qs = (jnp.concatenate([q_in] * GLA_HEADS, axis=0) * hm6_ref[...]).astype(BF16)
        att = _dot_nt(qs, k_in.astype(BF16))
        col_all = jnp.concatenate([acum[:, h * 64:h * 64 + C] for h in range(SSM_HEADS)], axis=0)
        lmat = jnp.exp(jnp.where(tril6, col_all - row_all, -jnp.inf))
        cacc = conf_taps(jnp.broadcast_to(cbias_ref[...], (C, CONF_CH)), 0, 10)

        att = jnp.where(tril6, att, 0.0)
        o_intra = _per_head_matmul(att.astype(BF16), vb, C)
        y_intra = _per_head_matmul((cb_all * lmat).astype(BF16), xdtb, C)
        st = st_scr[...]
        o = o_intra + _dot_nt(q_in.astype(BF16), st.astype(BF16))
        ht = ht_scr[...]
        y_inter = _dot(cm.astype(BF16), ht.astype(BF16)) * jnp.exp(acum)
        cacc = conf_taps(cacc, 10, 20)
        kv = _dot_tn(vb, k_out.astype(BF16))
        st_scr[...] = st * jnp.exp(blast) + kv * stmask_ref[...]
        wx = xdt * jnp.exp(alast - acum)
        stn = _dot_tn(bm.astype(BF16), wx.astype(BF16))
        ht_scr[...] = ht * jnp.exp(alast) + stn * htmask_ref[...]

        ms_o = seg_mean(o * o, bhead_ref, 1.0 / GLA_DV)
        y = y_intra + y_inter + dskip_ref[...] * xs
        y = y * _silu(P(Z0, SSM_WIDTH))
        ms_y = seg_mean(y * y, bgroup_ref, 1.0 / (SSM_HPG * SSM_HEADDIM))
        cacc = conf_taps(cacc, 20, CONF_K)
        mu = jnp.mean(cacc, axis=-1, keepdims=True)
        xc = cacc - mu
        var = jnp.mean(xc * xc, axis=-1, keepdims=True)
        cval = _silu(xc * lax.rsqrt(var + LN_EPS) * lnw_ref[...] + lnb_ref[...])
        mix_scr[pl.ds(r0, C), GLA_WIDTH:GLA_WIDTH + CONF_CH] = cval.astype(mix_scr.dtype)
        o = o * lax.rsqrt(ms_o + EPS) * gnorm_ref[...] * _silu(P(GO0, GLA_WIDTH))
        mix_scr[pl.ds(r0, C), 0:GLA_WIDTH] = o.astype(mix_scr.dtype)
        y = y * lax.rsqrt(ms_y + EPS) * snorm_ref[...]
        mix_scr[pl.ds(r0, C), GLA_WIDTH + CONF_CH:D_MODEL] = y.astype(mix_scr.dtype)

    def load_conv_hist(hconf, hsc):
        hist = jnp.concatenate([jnp.zeros((CONF_BASE - (CONF_K - 1), CONF_CH), F32), hconf], axis=0)
        ubuf[0, 0:CONF_BASE, :] = hist
        for s in range(1, SUBLANE):
            ubuf[s, 0:CONF_BASE - SUBLANE, :] = hist[s:s + CONF_BASE - SUBLANE, :]
        xbuf[0:SSMC_BASE, :] = jnp.concatenate(
            [jnp.zeros((SSMC_BASE - (SSM_CONV - 1), SSM_XBC), F32), hsc], axis=0)

    def load_state(j, first):
        if first:
            s = sgla_ref[j]
            rows = []
            for h in range(GLA_HEADS):
                parts = []
                if h > 0:
                    parts.append(jnp.zeros((GLA_DK, h * GLA_DV), F32))
                parts.append(s[h * GLA_DK:(h + 1) * GLA_DK, :])
                if h < GLA_HEADS - 1:
                    parts.append(jnp.zeros((GLA_DK, (GLA_HEADS - 1 - h) * GLA_DV), F32))
                rows.append(jnp.concatenate(parts, axis=1))
            rows.append(jnp.zeros((QK_PAD - GLA_QK, GLA_WIDTH), F32))
            st_scr[...] = jnp.concatenate(rows, axis=0).T
            hs = sssm_ref[j]
            rows = []
            for h in range(SSM_HEADS):
                blk = hs[h * SSM_HEADDIM:(h + 1) * SSM_HEADDIM, :]
                z = jnp.zeros((SSM_HEADDIM, SSM_STATE), F32)
                rows.append(jnp.concatenate([blk, z] if h // SSM_HPG == 0 else [z, blk], axis=1))
            ht_scr[...] = jnp.concatenate(rows, axis=0).T
            load_conv_hist(sconf_ref[j], ssc_ref[j])
        else:
            load_conv_hist(oconf_ref[j], osc_ref[j])

    def store_state(j):
        stt = st_scr[...].T
        for h in range(GLA_HEADS):
            ogla_ref[j, h * GLA_DK:(h + 1) * GLA_DK, :] = stt[h * GLA_DK:(h + 1) * GLA_DK, h * GLA_DV:(h + 1) * GLA_DV]
        htt = ht_scr[...].T
        for h in range(SSM_HEADS):
            g = h // SSM_HPG
            ossm_ref[j, h * SSM_HEADDIM:(h + 1) * SSM_HEADDIM, :] = (
                htt[h * SSM_HEADDIM:(h + 1) * SSM_HEADDIM, g * SSM_STATE:(g + 1) * SSM_STATE])

    def seq_body(j):
        if nl == 1:
            load_state(j, True)
        else:
            @pl.when(l == 0)
            def _():
                load_state(j, True)

            @pl.when(l > 0)
            def _():
                load_state(j, False)

        if nchunk == 1:
            chunk(pl.multiple_of(j * T, SUBLANE), 0)
        else:
            assert bb == 1
            cpg = nchunk // ngroups
            for ci in range(nchunk):
                chunk(ci * C, ci * C)
                if (ci + 1) % cpg == 0:
                    out_proj(ci // cpg)

        oconf_ref[j] = ubuf[0, T + CONF_BASE - (CONF_K - 1):T + CONF_BASE, :]
        osc_ref[j] = xbuf[T + SSMC_BASE - (SSM_CONV - 1):T + SSMC_BASE, :]
        if nl == 1:
            store_state(j)
        else:
            @pl.when(l == nl - 1)
            def _():
                store_state(j)

    if bb == 1:
        seq_body(0)
    else:
        def seq_loop(j, carry):
            seq_body(j)
            return carry
        lax.fori_loop(0, bb, seq_loop, 0)

    if nchunk == 1:
        out_proj(0)


def _np_consts(C):
    i = np.arange(C)
    tril = (i[None, :] <= i[:, None]).astype(np.float32)
    tril6 = np.tile(tril, (6, 1))
    hm6 = np.zeros((6 * C, QK_PAD), np.float32)
    e6 = np.zeros((6 * C, GLA_WIDTH), np.float32)
    gm6 = np.zeros((6 * C, SSM_GN), np.float32)
    for h in range(6):
        hm6[h * C:(h + 1) * C, h * GLA_DK:(h + 1) * GLA_DK] = 1.0
        e6[h * C:(h + 1) * C, h * 64] = 1.0
        g = h // SSM_HPG
        gm6[h * C:(h + 1) * C, g * SSM_STATE:(g + 1) * SSM_STATE] = 1.0
    stmask = np.zeros((GLA_WIDTH, QK_PAD), np.float32)
    htmask = np.zeros((SSM_GN, SSM_WIDTH), np.float32)
    bhead = np.zeros((GLA_WIDTH, GLA_WIDTH), np.float32)
    bgroup = np.zeros((SSM_WIDTH, SSM_WIDTH), np.float32)
    for h in range(6):
        stmask[h * 64:(h + 1) * 64, h * 32:(h + 1) * 32] = 1.0
        g = h // SSM_HPG
        htmask[g * 64:(g + 1) * 64, h * 64:(h + 1) * 64] = 1.0
        bhead[h * 64:(h + 1) * 64, h * 64:(h + 1) * 64] = 1.0
    gw = SSM_HPG * SSM_HEADDIM
    for g in range(SSM_GROUPS):
        bgroup[g * gw:(g + 1) * gw, g * gw:(g + 1) * gw] = 1.0
    return (jnp.asarray(tril, BF16), jnp.asarray(tril6, F32), jnp.asarray(hm6, F32), jnp.asarray(e6, BF16),
            jnp.asarray(gm6, F32), jnp.asarray(stmask, F32), jnp.asarray(htmask, F32),
            jnp.asarray(bhead, BF16), jnp.asarray(bgroup, BF16))


def _full_spec(a):
    nd = a.ndim
    return pl.BlockSpec(a.shape, lambda i, l: (0,) * nd, pipeline_mode=pl.Buffered(1))


def _mixer_call(x2d, states, wts, Bn, L, bb, T, C):
    assert L % T == 0 and T % C == 0 and Bn % bb == 0 and C % SUBLANE == 0
    nl = L // T
    assert bb == 1 or nl == 1
    nb = Bn // bb
    R = bb * T
    consts = _np_consts(C)
    sgla, sconf, ssc, sssm = states
    x_spec = pl.BlockSpec((R, D_MODEL), lambda i, l: (i * nl + l, 0))

    def st_spec(a):
        return pl.BlockSpec((bb,) + a.shape[1:], lambda i, l: (i, 0, 0))

    in_specs = ([x_spec] + [st_spec(a) for a in states] + [_full_spec(a) for a in wts]
                + [_full_spec(a) for a in consts])
    out_shape = ([jax.ShapeDtypeStruct((Bn * L, D_MODEL), F32)]
                 + [jax.ShapeDtypeStruct(a.shape, F32) for a in states])
    out_specs = [x_spec] + [st_spec(a) for a in states]
    scratch = [
        pltpu.VMEM((R, NP), F32),
        pltpu.VMEM((GLA_WIDTH, QK_PAD), F32),
        pltpu.VMEM((SSM_GN, SSM_WIDTH), F32),
        pltpu.VMEM((SUBLANE, CONF_BASE + T, CONF_CH), F32),
        pltpu.VMEM((SSMC_BASE + T, SSM_XBC), F32),
        pltpu.VMEM((R, D_MODEL), BF16 if C % (2 * SUBLANE) == 0 else F32),
    ]
    return pl.pallas_call(
        functools.partial(_mixer_kernel, C, T, bb, nl),
        grid=(nb, nl),
        in_specs=in_specs,
        out_specs=out_specs,
        out_shape=out_shape,
        scratch_shapes=scratch,
        compiler_params=pltpu.CompilerParams(
            dimension_semantics=("arbitrary", "arbitrary"), vmem_limit_bytes=VMEM_LIMIT_BYTES),
        name="mixer",
    )(x2d, sgla, sconf, ssc, sssm, *wts, *consts)


def _ffn_kernel(T, bb, nl, final,
                x_ref, sffn_ref, n2_ref, wa_ref, wg_ref, fcw_ref, fcb_ref, wo_ref, fn_ref,
                xo_ref, offn_ref, abuf, act):
    l = pl.program_id(1)
    R = bb * T
    x = x_ref[...]
    xn = (x * lax.rsqrt(jnp.mean(x * x, axis=-1, keepdims=True) + EPS) * n2_ref[...]).astype(BF16)
    lo = FFNC_BASE - (FFN_CONV - 1)
    if nl == 1:
        abuf[:, lo:FFNC_BASE, :] = sffn_ref[...]
    else:
        @pl.when(l == 0)
        def _():
            abuf[:, lo:FFNC_BASE, :] = sffn_ref[...]

        @pl.when(l > 0)
        def _():
            abuf[:, lo:FFNC_BASE, :] = offn_ref[...]
    f0 = 0
    for fw in FFN_SPLITS:
        a = _dot(xn, wa_ref[:, f0:f0 + fw])
        g = _dot(xn, wg_ref[:, f0:f0 + fw])
        abuf[:, FFNC_BASE:FFNC_BASE + T, f0:f0 + fw] = a.reshape(bb, T, fw)
        ac = fcb_ref[:, f0:f0 + fw][None]
        for kk in range(FFN_CONV):
            ac = ac + fcw_ref[kk:kk + 1, f0:f0 + fw][None] * abuf[:, lo + kk:lo + kk + T, f0:f0 + fw]
        offn_ref[:, :, f0:f0 + fw] = abuf[:, T + lo:T + FFNC_BASE, f0:f0 + fw]
        act[:, f0:f0 + fw] = (_silu(ac).reshape(R, fw) * g).astype(BF16)
        f0 += fw
    y = x + _dot(act[...], wo_ref[...])
    if final:
        y = y * lax.rsqrt(jnp.mean(y * y, axis=-1, keepdims=True) + EPS) * fn_ref[...]
    xo_ref[...] = y


def _ffn_call(x2d, sffn, wts, Bn, L, bb, T, final):
    assert L % T == 0 and Bn % bb == 0
    nl = L // T
    assert bb == 1 or nl == 1
    nb = Bn // bb
    R = bb * T
    x_spec = pl.BlockSpec((R, D_MODEL), lambda i, l: (i * nl + l, 0))
    s_spec = pl.BlockSpec((bb, FFN_CONV - 1, D_FF), lambda i, l: (i, 0, 0))
    return pl.pallas_call(
        functools.partial(_ffn_kernel, T, bb, nl, final),
        grid=(nb, nl),
        in_specs=[x_spec, s_spec] + [_full_spec(a) for a in wts],
        out_specs=[x_spec, s_spec],
        out_shape=[jax.ShapeDtypeStruct((Bn * L, D_MODEL), F32),
                   jax.ShapeDtypeStruct((Bn, FFN_CONV - 1, D_FF), F32)],
        scratch_shapes=[pltpu.VMEM((bb, FFNC_BASE + T, D_FF), F32),
                        pltpu.VMEM((R, D_FF), BF16)],
        compiler_params=pltpu.CompilerParams(
            dimension_semantics=("arbitrary", "arbitrary"), vmem_limit_bytes=VMEM_LIMIT_BYTES),
        name="ffn",
    )(x2d, sffn, *wts)


def _prep_layer(l, norm1, w_in, w_gk, b_gk, gla_norm, conf_w, conf_b, conf_ln_w, conf_ln_b,
                ssm_conv_w, ssm_conv_b, dt_bias, a_log, d_skip, ssm_norm, w_out,
                norm2, w_ffn_in, ffn_conv_w, ffn_conv_b, w_ffn_out):
    offs = np.concatenate([[0], np.cumsum(IN_SIZES)])
    w = w_in[l]
    seg = lambda i: w[:, int(offs[i]):int(offs[i + 1])]
    zpad = lambda n: jnp.zeros((D_MODEL, n), w.dtype)
    w_in_p = jnp.concatenate([
        seg(0), zpad(QK_PAD - GLA_QK), seg(1), zpad(QK_PAD - GLA_QK), seg(2), seg(3),
        seg(5), seg(6), seg(7), seg(8), jnp.repeat(seg(9), SSM_HEADDIM, axis=1),
        seg(4), zpad(LANE - GLA_RANK)], axis=1).astype(BF16)
    assert w_in_p.shape == (D_MODEL, NP)
    row = lambda v: v.reshape(1, -1).astype(F32)
    wgk_p = jnp.zeros((LANE, QK_PAD), F32).at[:GLA_RANK, :GLA_QK].set(w_gk[l]).astype(BF16)
    bgk_p = jnp.zeros((1, QK_PAD), F32).at[0, :GLA_QK].set(b_gk[l])
    rep64 = lambda v: row(jnp.repeat(v, SSM_HEADDIM))
    mixer_w = (row(norm1[l]), w_in_p, wgk_p, bgk_p, row(jnp.tile(gla_norm[l], GLA_HEADS)),
               conf_w[l].astype(F32), row(conf_b[l]), row(conf_ln_w[l]), row(conf_ln_b[l]),
               ssm_conv_w[l].astype(F32), row(ssm_conv_b[l]), rep64(dt_bias[l]), rep64(a_log[l]),
               rep64(d_skip[l]), row(ssm_norm[l]), w_out[l].astype(BF16))
    ffn_w = (row(norm2[l]), w_ffn_in[l][:, :D_FF].astype(BF16), w_ffn_in[l][:, D_FF:].astype(BF16),
             ffn_conv_w[l].astype(F32), row(ffn_conv_b[l]), w_ffn_out[l].astype(BF16))
    return mixer_w, ffn_w


def _run_group(x, states, layer_w, final_norm, bb, T, C):
    Bn, L, _ = x.shape
    x2d = x.reshape(Bn * L, D_MODEL)
    fn = final_norm.reshape(1, -1).astype(F32)
    new_states = []
    for l in range(DEPTH):
        mixer_w, ffn_w = layer_w[l]
        sgla, sconf, ssc, sssm, sffn = states[l]
        x2d, ngla, nconf, nsc, nssm = _mixer_call(x2d, (sgla, sconf, ssc, sssm), mixer_w, Bn, L, bb, T, C)
        x2d, nffn = _ffn_call(x2d, sffn, ffn_w + (fn,), Bn, L, bb, T, final=(l == DEPTH - 1))
        new_states.append((ngla, nconf, nsc, nssm, nffn))
    return x2d.reshape(Bn, L, D_MODEL), new_states


def kernel(x_prompt, x_sample, state_gla, state_conf_conv, state_ssm_conv, state_ssm, state_ffn_conv, norm1, w_in, w_gk, b_gk, gla_norm, conf_w, conf_b, conf_ln_w, conf_ln_b, ssm_conv_w, ssm_conv_b, dt_bias, a_log, d_skip, ssm_norm, w_out, norm2, w_ffn_in, ffn_conv_w, ffn_conv_b, w_ffn_out, final_norm):
    weights = (norm1, w_in, w_gk, b_gk, gla_norm, conf_w, conf_b, conf_ln_w, conf_ln_b,
               ssm_conv_w, ssm_conv_b, dt_bias, a_log, d_skip, ssm_norm, w_out,
               norm2, w_ffn_in, ffn_conv_w, ffn_conv_b, w_ffn_out)
    layer_w = [_prep_layer(l, *weights) for l in range(DEPTH)]
    Bp, Lp, _ = x_prompt.shape
    Bs, Ls, _ = x_sample.shape

    def zero_states(bn):
        return (jnp.zeros((bn, GLA_QK, GLA_DV), F32), jnp.zeros((bn, CONF_K - 1, CONF_CH), F32),
                jnp.zeros((bn, SSM_CONV - 1, SSM_XBC), F32), jnp.zeros((bn, SSM_WIDTH, SSM_STATE), F32),
                jnp.zeros((bn, FFN_CONV - 1, D_FF), F32))

    p_states = [zero_states(Bp) for _ in range(DEPTH)]
    s_states = [(state_gla[l].reshape(Bs, GLA_QK, GLA_DV), state_conf_conv[l], state_ssm_conv[l],
                 state_ssm[l].reshape(Bs, SSM_WIDTH, SSM_STATE), state_ffn_conv[l]) for l in range(DEPTH)]

    tp = min(512, Lp)
    y_p, np_states = _run_group(x_prompt, p_states, layer_w, final_norm, 1, tp, math.gcd(tp, GLA_CHUNK))
    bbs = min(16, Bs)
    y_s, ns_states = _run_group(x_sample, s_states, layer_w, final_norm, bbs, Ls, math.gcd(Ls, GLA_CHUNK))

    def pack(sts, bn):
        gla = jnp.stack([s[0].reshape(bn, GLA_HEADS, GLA_DK, GLA_DV) for s in sts])
        conf = jnp.stack([s[1] for s in sts])
        sconv = jnp.stack([s[2] for s in sts])
        ssm = jnp.stack([s[3].reshape(bn, SSM_HEADS, SSM_HEADDIM, SSM_STATE) for s in sts])
        ffn = jnp.stack([s[4] for s in sts])
        return gla, conf, sconv, ssm, ffn

    return (y_p, y_s) + pack(np_states, Bp) + pack(ns_states, Bs)
```

```python
import collections
import functools
import math

import numpy as np
import jax
import jax.numpy as jnp
from jax import lax
from jax.experimental import pallas as pl
from jax.experimental.pallas import tpu as pltpu

F32 = jnp.float32
BF16 = jnp.bfloat16

D_MODEL = 1024
DEPTH = 2
GLA_HEADS = 6
GLA_DK = 32
GLA_DV = 64
GLA_QK = GLA_HEADS * GLA_DK
GLA_WIDTH = GLA_HEADS * GLA_DV
GLA_RANK = 16
GLA_GATE_NORM = 16.0
GLA_CHUNK = 64
CONF_CH = 256
CONF_K = 31
LN_EPS = 1e-5
SSM_WIDTH = 384
SSM_HEADDIM = 64
SSM_HEADS = 6
SSM_GROUPS = 2
SSM_HPG = 3
SSM_STATE = 64
SSM_GN = SSM_GROUPS * SSM_STATE
SSM_XBC = SSM_WIDTH + 2 * SSM_GN
SSM_CONV = 4
SSM_CHUNK = 64
D_FF = 2816
FFN_CONV = 3
EPS = 1e-6
IN_SIZES = (GLA_QK, GLA_QK, GLA_WIDTH, GLA_WIDTH, GLA_RANK, CONF_CH, CONF_CH, SSM_WIDTH, SSM_XBC, SSM_HEADS)

LANE = 128
SUBLANE = 8
MXU_DIM = 256
VMEM_LIMIT_BYTES = 56 * 1024 * 1024

QK_PAD = 256
Q0 = 0
K0 = Q0 + QK_PAD
V0 = K0 + QK_PAD
GO0 = V0 + GLA_WIDTH
CA0 = GO0 + GLA_WIDTH
CB0 = CA0 + CONF_CH
Z0 = CB0 + CONF_CH
XBC0 = Z0 + SSM_WIDTH
DT0 = XBC0 + SSM_XBC
GLR0 = DT0 + SSM_WIDTH
NP = GLR0 + LANE

SPLIT_PARTS = 2

FFN_SPLITS = (768, 768, 768, 512)

CONF_BASE = 32
SSMC_BASE = 8
FFNC_BASE = 8

CHUNK_STAGE_LAG = 2
PROJ_PIECES_PER_TICK = 2


def _dot(a, b):
    return jnp.dot(a, b, preferred_element_type=F32)


def _dot_nt(a, b):
    return lax.dot_general(a, b, (((1,), (1,)), ((), ())), preferred_element_type=F32)


def _dot_tn(a, b):
    return lax.dot_general(a, b, (((0,), (0,)), ((), ())), preferred_element_type=F32)


def _split_bf16(a, parts):
    out = []
    r = a
    for i in range(parts):
        p = r.astype(BF16)
        out.append(p)
        if i + 1 < parts:
            r = r - p.astype(F32)
    return out


def _sigmoid(x):
    return 1.0 / (1.0 + jnp.exp(-x))


def _silu(x):
    return x * _sigmoid(x)


def _softplus(x):
    return jnp.maximum(x, 0.0) + jnp.log(1.0 + jnp.exp(-jnp.abs(x)))


def _log_sigmoid(x):
    return jnp.minimum(x, 0.0) - jnp.log(1.0 + jnp.exp(-jnp.abs(x)))


def _rms_scale(x):
    return x * lax.rsqrt(jnp.mean(x * x, axis=-1, keepdims=True) + EPS)


def _per_head_matmul(stacked, rhs, c):
    first_head = lax.broadcasted_iota(jnp.int32, (c, LANE), 1) < 64
    cols = []
    for col in range(3):
        two = _dot(stacked[2 * col * c:(2 * col + 2) * c, :], rhs[:, col * LANE:(col + 1) * LANE])
        cols.append(jnp.where(first_head, two[0:c, :], two[c:2 * c, :]))
    return jnp.concatenate(cols, axis=1)


def _run_staged(chunk_gens, lag, fillers=None, before_start=None, after_finish=None):
    pending = list(enumerate(chunk_gens))
    active = []
    tick = 0
    while pending or active:
        if pending and tick % lag == 0:
            i, g = pending.pop(0)
            if before_start is not None:
                before_start(i)
            active.append((i, g))
        for item in list(active):
            i, g = item
            try:
                next(g)
            except StopIteration:
                active.remove(item)
                if after_finish is not None:
                    after_finish(i)
        if fillers is not None:
            next(fillers, None)
        tick += 1


MixerCfg = collections.namedtuple("MixerCfg", "C T bb nl zero_init n_alias")


def _mixer_kernel(cfg, *refs):
    C, T, bb, nl = cfg.C, cfg.T, cfg.bb, cfg.nl
    refs = list(refs)
    x_ref = refs.pop(0)
    if cfg.zero_init:
        sgla_ref = sconf_ref = ssc_ref = sssm_ref = None
    else:
        sgla_ref, sconf_ref, ssc_ref, sssm_ref = refs[:4]
        del refs[:4]
    (n1_ref, win_ref, wgk_ref, bgk_ref, gnorm_ref, cw_ref, cbias_ref, lnw_ref, lnb_ref,
     scw_ref, scb_ref, dtb_ref, alog_ref, dskip_ref, snorm_ref, wout_ref) = refs[:16]
    del refs[:16]
    (tril_ref, tril6_ref, hm6_ref, e6_ref, gm6_ref, stmask_ref, htmask_ref, bhead_ref, bgroup_ref) = refs[:9]
    del refs[:9]
    del refs[:cfg.n_alias]
    xo_ref, ogla_ref, oconf_ref, osc_ref, ossm_ref = refs[:5]
    proj_scr, xn_scr, st_scr, ht_scr, ubuf, xbuf, mix_scr = refs[5:]

    l = pl.program_id(1)
    nchunk = T // C
    ngroups = 2 if (bb == 1 and nchunk % 2 == 0) else 1
    RG = bb * T // ngroups
    NPIECE = NP // MXU_DIM

    def in_proj_full(g):
        rows = slice(g * RG, (g + 1) * RG)
        xn = _rms_scale(x_ref[rows, :]) * n1_ref[...]
        proj_scr[rows, :] = _dot(xn.astype(BF16), win_ref[...])

    def in_proj_pieces(g):
        rows = slice(g * RG, (g + 1) * RG)
        xn_scr[...] = (_rms_scale(x_ref[rows, :]) * n1_ref[...]).astype(BF16)
        for p in range(NPIECE):
            cols = slice(p * MXU_DIM, (p + 1) * MXU_DIM)
            proj_scr[rows, cols] = _dot(xn_scr[...], win_ref[:, cols])
            if (p + 1) % PROJ_PIECES_PER_TICK == 0:
                yield

    def out_proj_pieces(g):
        rows = slice(g * RG, (g + 1) * RG)
        for p in range(D_MODEL // MXU_DIM):
            cols = slice(p * MXU_DIM, (p + 1) * MXU_DIM)
            xo_ref[rows, cols] = x_ref[rows, cols] + _dot(mix_scr[rows, :].astype(BF16), wout_ref[:, cols])
            yield

    tril = tril_ref[...]
    tril6 = tril6_ref[...] > 0.5

    def cumsum_rows(a):
        acc = None
        for p in _split_bf16(a, SPLIT_PARTS):
            t = _dot(tril, p)
            acc = t if acc is None else acc + t
        return acc

    def seg_mean(a, ones_ref, inv_n):
        acc = None
        for p in _split_bf16(a, SPLIT_PARTS):
            t = _dot(p, ones_ref[...])
            acc = t if acc is None else acc + t
        return acc * inv_n

    def chunk_stages(r0, c0, slot):
        def P(off, width):
            return proj_scr[pl.ds(r0, C), off:off + width]

        gk_pre = _dot(P(GLR0, LANE).astype(BF16), wgk_ref[...]) + bgk_ref[...]
        xbc = P(XBC0, SSM_XBC)
        ext = jnp.concatenate([xbuf[slot, pl.ds(c0, SUBLANE), :], xbc], axis=0)
        xbuf[slot, pl.ds(SSMC_BASE + c0, C), :] = xbc
        acc = jnp.broadcast_to(scb_ref[...], (C, SSM_XBC))
        for kk in range(SSM_CONV):
            lo = SUBLANE - (SSM_CONV - 1) + kk
            acc = acc + scw_ref[kk:kk + 1, :] * ext[lo:lo + C, :]
        xcv = _silu(acc)
        xs = xcv[:, 0:SSM_WIDTH]
        bm = xcv[:, SSM_WIDTH:SSM_WIDTH + SSM_GN]
        cm = xcv[:, SSM_WIDTH + SSM_GN:SSM_XBC]
        dtf = _softplus(P(DT0, SSM_WIDTH) + dtb_ref[...])
        af = dtf * (-jnp.exp(alog_ref[...]))
        yield

        gk = _log_sigmoid(gk_pre) * (1.0 / GLA_GATE_NORM)
        acum = cumsum_rows(af)
        b = cumsum_rows(gk)
        u = P(CA0, CONF_CH) * _sigmoid(P(CB0, CONF_CH))
        b0 = CONF_BASE + c0
        uext = jnp.concatenate([ubuf[slot, 0, pl.ds(b0 - SUBLANE, SUBLANE), :], u], axis=0)
        ubuf[slot, 0, pl.ds(b0, C), :] = u
        for s in range(1, SUBLANE):
            ubuf[slot, s, pl.ds(b0 - SUBLANE, C), :] = uext[s:s + C, :]
        xdt = xs * dtf
        xdtb = xdt.astype(BF16)
        cms = (jnp.concatenate([cm] * SSM_HEADS, axis=0) * gm6_ref[...]).astype(BF16)
        cb_all = _dot_nt(cms, bm.astype(BF16))

        def conf_taps(acc, k_lo, k_hi):
            for kk in range(k_lo, k_hi):
                s = (kk - (CONF_K - 1)) % SUBLANE
                off = kk - (CONF_K - 1) - s
                acc = acc + cw_ref[kk:kk + 1, :] * ubuf[slot, s, pl.ds(b0 + off, C), :]
            return acc
        yield

        alast = acum[C - 1:C, :]
        if C == SSM_CHUNK:
            acum_t = acum.T
            row_all = jnp.concatenate(
                [jnp.broadcast_to(acum_t[h * 64:h * 64 + 1, :], (C, C)) for h in range(SSM_HEADS)], axis=0)
        else:
            row_all = None
            for p in _split_bf16(acum, SPLIT_PARTS):
                t = _dot_nt(e6_ref[...], p)
                row_all = t if row_all is None else row_all + t
        blast = b[C - 1:C, :]
        q_in = P(Q0, QK_PAD) * (GLA_DK ** -0.5) * jnp.exp(b)
        k = P(K0, QK_PAD)
        k_in = k * jnp.exp(-b)
        k_out = k * jnp.exp(blast - b)
        vb = P(V0, GLA_WIDTH).astype(BF16)
        qs = (jnp.concatenate([q_in] * GLA_HEADS, axis=0) * hm6_ref[...]).astype(BF16)
        att = _dot_nt(qs, k_in.astype(BF16))
        col_all = jnp.concatenate([acum[:, h * 64:h * 64 + C] for h in range(SSM_HEADS)], axis=0)
        lmat = jnp.exp(jnp.where(tril6, col_all - row_all, -jnp.inf))
        cacc = conf_taps(jnp.broadcast_to(cbias_ref[...], (C, CONF_CH)), 0, 10)
        yield

        att = jnp.where(tril6, att, 0.0)
        o_intra = _per_head_matmul(att.astype(BF16), vb, C)
        y_intra = _per_head_matmul((cb_all * lmat).astype(BF16), xdtb, C)
        st = st_scr[slot]
        o = o_intra + _dot_nt(q_in.astype(BF16), st.astype(BF16))
        ht = ht_scr[slot]
        y_inter = _dot(cm.astype(BF16), ht.astype(BF16)) * jnp.exp(acum)
        cacc = conf_taps(cacc, 10, 20)
        kv = _dot_tn(vb, k_out.astype(BF16))
        st_scr[slot] = st * jnp.exp(blast) + kv * stmask_ref[...]
        wx = xdt * jnp.exp(alast - acum)
        stn = _dot_tn(bm.astype(BF16), wx.astype(BF16))
        ht_scr[slot] = ht * jnp.exp(alast) + stn * htmask_ref[...]
        yield

        ms_o = seg_mean(o * o, bhead_ref, 1.0 / GLA_DV)
        y = y_intra + y_inter + dskip_ref[...] * xs
        y = y * _silu(P(Z0, SSM_WIDTH))
        ms_y = seg_mean(y * y, bgroup_ref, 1.0 / (SSM_HPG * SSM_HEADDIM))
        cacc = conf_taps(cacc, 20, CONF_K)
        mu = jnp.mean(cacc, axis=-1, keepdims=True)
        xc = cacc - mu
        var = jnp.mean(xc * xc, axis=-1, keepdims=True)
        cval = _silu(xc * lax.rsqrt(var + LN_EPS) * lnw_ref[...] + lnb_ref[...])
        mix_scr[pl.ds(r0, C), GLA_WIDTH:GLA_WIDTH + CONF_CH] = cval.astype(mix_scr.dtype)
        o = o * lax.rsqrt(ms_o + EPS) * gnorm_ref[...] * _silu(P(GO0, GLA_WIDTH))
        mix_scr[pl.ds(r0, C), 0:GLA_WIDTH] = o.astype(mix_scr.dtype)
        y = y * lax.rsqrt(ms_y + EPS) * snorm_ref[...]
        mix_scr[pl.ds(r0, C), GLA_WIDTH + CONF_CH:D_MODEL] = y.astype(mix_scr.dtype)

    def load_conv_hist(slot, hconf, hsc):
        hist = jnp.concatenate([jnp.zeros((CONF_BASE - (CONF_K - 1), CONF_CH), F32), hconf], axis=0)
        ubuf[slot, 0, 0:CONF_BASE, :] = hist
        for s in range(1, SUBLANE):
            ubuf[slot, s, 0:CONF_BASE - SUBLANE, :] = hist[s:s + CONF_BASE - SUBLANE, :]
        xbuf[slot, 0:SSMC_BASE, :] = jnp.concatenate(
            [jnp.zeros((SSMC_BASE - (SSM_CONV - 1), SSM_XBC), F32), hsc], axis=0)

    def load_state(j, slot, first):
        if not first:
            load_conv_hist(slot, oconf_ref[j], osc_ref[j])
        elif cfg.zero_init:
            st_scr[slot] = jnp.zeros((GLA_WIDTH, QK_PAD), F32)
            ht_scr[slot] = jnp.zeros((SSM_GN, SSM_WIDTH), F32)
            load_conv_hist(slot, jnp.zeros((CONF_K - 1, CONF_CH), F32), jnp.zeros((SSM_CONV - 1, SSM_XBC), F32))
        else:
            rows = []
            for h in range(GLA_HEADS):
                parts = []
                if h > 0:
                    parts.append(jnp.zeros((GLA_DK, h * GLA_DV), F32))
                parts.append(sgla_ref[j, h])
                if h < GLA_HEADS - 1:
                    parts.append(jnp.zeros((GLA_DK, (GLA_HEADS - 1 - h) * GLA_DV), F32))
                rows.append(jnp.concatenate(parts, axis=1))
            rows.append(jnp.zeros((QK_PAD - GLA_QK, GLA_WIDTH), F32))
            st_scr[slot] = jnp.concatenate(rows, axis=0).T
            rows = []
            for h in range(SSM_HEADS):
                z = jnp.zeros((SSM_HEADDIM, SSM_STATE), F32)
                rows.append(jnp.concatenate([sssm_ref[j, h], z] if h // SSM_HPG == 0 else [z, sssm_ref[j, h]], axis=1))
            ht_scr[slot] = jnp.concatenate(rows, axis=0).T
            load_conv_hist(slot, sconf_ref[j], ssc_ref[j])

    def store_conv_state(j, slot):
        oconf_ref[j] = ubuf[slot, 0, T + CONF_BASE - (CONF_K - 1):T + CONF_BASE, :]
        osc_ref[j] = xbuf[slot, T + SSMC_BASE - (SSM_CONV - 1):T + SSMC_BASE, :]

    def store_state(j, slot):
        stt = st_scr[slot].T
        for h in range(GLA_HEADS):
            ogla_ref[j, h] = stt[h * GLA_DK:(h + 1) * GLA_DK, h * GLA_DV:(h + 1) * GLA_DV]
        htt = ht_scr[slot].T
        for h in range(SSM_HEADS):
            g = h // SSM_HPG
            ossm_ref[j, h] = htt[h * SSM_HEADDIM:(h + 1) * SSM_HEADDIM, g * SSM_STATE:(g + 1) * SSM_STATE]

    if bb == 1:
        in_proj_full(0)
        if nl == 1:
            load_state(0, 0, True)
        else:
            @pl.when(l == 0)
            def _():
                load_state(0, 0, True)

            @pl.when(l > 0)
            def _():
                load_state(0, 0, False)

        cpg = nchunk // ngroups
        work = collections.deque()
        if ngroups == 2:
            work.append(in_proj_pieces(1))

        def drain_work():
            while work:
                for _ in work.popleft():
                    pass

        def before_start(i):
            if ngroups == 2 and i == cpg:
                drain_work()

        def after_finish(i):
            if (i + 1) % cpg == 0:
                work.append(out_proj_pieces(i // cpg))

        def work_pieces():
            while True:
                if work:
                    try:
                        next(work[0])
                    except StopIteration:
                        work.popleft()
                yield

        _run_staged([chunk_stages(ci * C, ci * C, 0) for ci in range(nchunk)], CHUNK_STAGE_LAG,
                    fillers=work_pieces(), before_start=before_start, after_finish=after_finish)
        drain_work()

        store_conv_state(0, 0)
        if nl == 1:
            store_state(0, 0)
        else:
            @pl.when(l == nl - 1)
            def _():
                store_state(0, 0)
    else:
        assert nl == 1 and nchunk == 1
        in_proj_full(0)

        def seq_stages(j):
            slot = j % 2
            load_state(j, slot, True)
            yield
            yield from chunk_stages(j * T, 0, slot)
            store_conv_state(j, slot)
            store_state(j, slot)

        _run_staged([seq_stages(j) for j in range(bb)], 3)
        xo_ref[...] = x_ref[...] + _dot(mix_scr[...].astype(BF16), wout_ref[...])


def _np_consts(C):
    i = np.arange(C)
    tril = (i[None, :] <= i[:, None]).astype(np.float32)
    tril6 = np.tile(tril, (6, 1))
    hm6 = np.zeros((6 * C, QK_PAD), np.float32)
    e6 = np.zeros((6 * C, GLA_WIDTH), np.float32)
    gm6 = np.zeros((6 * C, SSM_GN), np.float32)
    for h in range(6):
        hm6[h * C:(h + 1) * C, h * GLA_DK:(h + 1) * GLA_DK] = 1.0
        e6[h * C:(h + 1) * C, h * 64] = 1.0
        g = h // SSM_HPG
        gm6[h * C:(h + 1) * C, g * SSM_STATE:(g + 1) * SSM_STATE] = 1.0
    stmask = np.zeros((GLA_WIDTH, QK_PAD), np.float32)
    htmask = np.zeros((SSM_GN, SSM_WIDTH), np.float32)
    bhead = np.zeros((GLA_WIDTH, GLA_WIDTH), np.float32)
    bgroup = np.zeros((SSM_WIDTH, SSM_WIDTH), np.float32)
    for h in range(6):
        stmask[h * 64:(h + 1) * 64, h * 32:(h + 1) * 32] = 1.0
        g = h // SSM_HPG
        htmask[g * 64:(g + 1) * 64, h * 64:(h + 1) * 64] = 1.0
        bhead[h * 64:(h + 1) * 64, h * 64:(h + 1) * 64] = 1.0
    gw = SSM_HPG * SSM_HEADDIM
    for g in range(SSM_GROUPS):
        bgroup[g * gw:(g + 1) * gw, g * gw:(g + 1) * gw] = 1.0
    return (jnp.asarray(tril, BF16), jnp.asarray(tril6, F32), jnp.asarray(hm6, F32), jnp.asarray(e6, BF16),
            jnp.asarray(gm6, F32), jnp.asarray(stmask, F32), jnp.asarray(htmask, F32),
            jnp.asarray(bhead, BF16), jnp.asarray(bgroup, BF16))


def _const_spec(a):
    nd = a.ndim
    return pl.BlockSpec(a.shape, lambda i, l: (0,) * nd, pipeline_mode=pl.Buffered(1))


def _layer_spec(a, layer, block=None, col=0):
    shape = a.shape[1:] if block is None else block
    return pl.BlockSpec((None,) + tuple(shape), lambda i, l: (layer, 0, col), pipeline_mode=pl.Buffered(1))


def _state_spec(a, layer, bb):
    nrest = a.ndim - 2
    return pl.BlockSpec((None, bb) + tuple(a.shape[2:]), lambda i, l: (layer, i) + (0,) * nrest)


def _mixer_call(x2d, states_in, states_prev, wts, layer, Bn, L, bb, T, C):
    assert L % T == 0 and T % C == 0 and Bn % bb == 0 and C % SUBLANE == 0
    nl = L // T
    assert bb == 1 or nl == 1
    nb = Bn // bb
    R = bb * T
    nslot = 1 if bb == 1 else 2
    consts = _np_consts(C)
    x_spec = pl.BlockSpec((R, D_MODEL), lambda i, l: (i * nl + l, 0))
    state_shapes = ((DEPTH, Bn, GLA_HEADS, GLA_DK, GLA_DV), (DEPTH, Bn, CONF_K - 1, CONF_CH),
                    (DEPTH, Bn, SSM_CONV - 1, SSM_XBC), (DEPTH, Bn, SSM_HEADS, SSM_HEADDIM, SSM_STATE))
    out_shape = ([jax.ShapeDtypeStruct((Bn * L, D_MODEL), F32)]
                 + [jax.ShapeDtypeStruct(s, F32) for s in state_shapes])
    out_specs = [x_spec] + [_state_spec(s, layer, bb) for s in out_shape[1:]]

    operands = [x2d]
    in_specs = [x_spec]
    if states_in is not None:
        operands += list(states_in)
        in_specs += [_state_spec(a, layer, bb) for a in states_in]
    operands += list(wts)
    in_specs += [_layer_spec(a, layer) for a in wts]
    operands += list(consts)
    in_specs += [_const_spec(a) for a in consts]
    aliases = {}
    if states_prev is not None:
        for k, a in enumerate(states_prev):
            aliases[len(operands)] = 1 + k
            operands.append(a)
            in_specs.append(pl.BlockSpec(memory_space=pl.ANY))
    cfg = MixerCfg(C, T, bb, nl, states_in is None, len(aliases))
    ngroups = 2 if (bb == 1 and (T // C) % 2 == 0) else 1
    scratch = [
        pltpu.VMEM((R, NP), F32),
        pltpu.VMEM((R // ngroups, D_MODEL), BF16),
        pltpu.VMEM((nslot, GLA_WIDTH, QK_PAD), F32),
        pltpu.VMEM((nslot, SSM_GN, SSM_WIDTH), F32),
        pltpu.VMEM((nslot, SUBLANE, CONF_BASE + T, CONF_CH), F32),
        pltpu.VMEM((nslot, SSMC_BASE + T, SSM_XBC), F32),
        pltpu.VMEM((R, D_MODEL), BF16 if C % (2 * SUBLANE) == 0 else F32),
    ]
    return pl.pallas_call(
        functools.partial(_mixer_kernel, cfg),
        grid=(nb, nl),
        in_specs=in_specs,
        out_specs=out_specs,
        out_shape=out_shape,
        scratch_shapes=scratch,
        input_output_aliases=aliases,
        compiler_params=pltpu.CompilerParams(
            dimension_semantics=("arbitrary", "arbitrary"), vmem_limit_bytes=VMEM_LIMIT_BYTES),
        name="mixer",
    )(*operands)


FfnCfg = collections.namedtuple("FfnCfg", "T bb nl final zero_init n_alias")


def _ffn_kernel(cfg, *refs):
    T, bb, nl = cfg.T, cfg.bb, cfg.nl
    refs = list(refs)
    x_ref = refs.pop(0)
    sffn_ref = None if cfg.zero_init else refs.pop(0)
    n2_ref, wa_ref, wg_ref, fcw_ref, fcb_ref, wo_ref, fn_ref = refs[:7]
    del refs[:7]
    del refs[:cfg.n_alias]
    xo_ref, offn_ref, abuf, act = refs

    l = pl.program_id(1)
    R = bb * T
    x = x_ref[...]
    xn = (_rms_scale(x) * n2_ref[...]).astype(BF16)
    lo = FFNC_BASE - (FFN_CONV - 1)

    def init_hist():
        if cfg.zero_init:
            abuf[:, lo:FFNC_BASE, :] = jnp.zeros((bb, FFN_CONV - 1, D_FF), F32)
        else:
            abuf[:, lo:FFNC_BASE, :] = sffn_ref[...]

    if nl == 1:
        init_hist()
    else:
        @pl.when(l == 0)
        def _():
            init_hist()

        @pl.when(l > 0)
        def _():
            abuf[:, lo:FFNC_BASE, :] = offn_ref[...]
    f0 = 0
    for fw in FFN_SPLITS:
        a = _dot(xn, wa_ref[:, f0:f0 + fw])
        g = _dot(xn, wg_ref[:, f0:f0 + fw])
        abuf[:, FFNC_BASE:FFNC_BASE + T, f0:f0 + fw] = a.reshape(bb, T, fw)
        ac = fcb_ref[:, f0:f0 + fw][None]
        for kk in range(FFN_CONV):
            ac = ac + fcw_ref[kk:kk + 1, f0:f0 + fw][None] * abuf[:, lo + kk:lo + kk + T, f0:f0 + fw]
        offn_ref[:, :, f0:f0 + fw] = abuf[:, T + lo:T + FFNC_BASE, f0:f0 + fw]
        act[:, f0:f0 + fw] = (_silu(ac).reshape(R, fw) * g).astype(BF16)
        f0 += fw
    y = x + _dot(act[...], wo_ref[...])
    if cfg.final:
        y = _rms_scale(y) * fn_ref[...]
    xo_ref[...] = y


def _ffn_call(x2d, state_in, state_prev, wts, layer, Bn, L, bb, T, final):
    assert L % T == 0 and Bn % bb == 0
    nl = L // T
    assert bb == 1 or nl == 1
    nb = Bn // bb
    R = bb * T
    n2, w_ffn_in, fcw, fcb, w_ffn_out, fn = wts
    x_spec = pl.BlockSpec((R, D_MODEL), lambda i, l: (i * nl + l, 0))
    st_shape = jax.ShapeDtypeStruct((DEPTH, Bn, FFN_CONV - 1, D_FF), F32)
    operands = [x2d]
    in_specs = [x_spec]
    if state_in is not None:
        operands.append(state_in)
        in_specs.append(_state_spec(state_in, layer, bb))
    operands += [n2, w_ffn_in, w_ffn_in, fcw, fcb, w_ffn_out, fn]
    in_specs += [_layer_spec(n2, layer),
                 _layer_spec(w_ffn_in, layer, block=(D_MODEL, D_FF), col=0),
                 _layer_spec(w_ffn_in, layer, block=(D_MODEL, D_FF), col=1),
                 _layer_spec(fcw, layer), _layer_spec(fcb, layer), _layer_spec(w_ffn_out, layer), _const_spec(fn)]
    aliases = {}
    if state_prev is not None:
        aliases[len(operands)] = 1
        operands.append(state_prev)
        in_specs.append(pl.BlockSpec(memory_space=pl.ANY))
    cfg = FfnCfg(T, bb, nl, final, state_in is None, len(aliases))
    return pl.pallas_call(
        functools.partial(_ffn_kernel, cfg),
        grid=(nb, nl),
        in_specs=in_specs,
        out_specs=[x_spec, _state_spec(st_shape, layer, bb)],
        out_shape=[jax.ShapeDtypeStruct((Bn * L, D_MODEL), F32), st_shape],
        scratch_shapes=[pltpu.VMEM((bb, FFNC_BASE + T, D_FF), F32),
                        pltpu.VMEM((R, D_FF), BF16)],
        input_output_aliases=aliases,
        compiler_params=pltpu.CompilerParams(
            dimension_semantics=("arbitrary", "arbitrary"), vmem_limit_bytes=VMEM_LIMIT_BYTES),
        name="ffn",
    )(*operands)


def _prep_weights(norm1, w_in, w_gk, b_gk, gla_norm, conf_w, conf_b, conf_ln_w, conf_ln_b,
                  ssm_conv_w, ssm_conv_b, dt_bias, a_log, d_skip, ssm_norm, w_out,
                  norm2, w_ffn_in, ffn_conv_w, ffn_conv_b, w_ffn_out, final_norm):
    offs = np.concatenate([[0], np.cumsum(IN_SIZES)])
    wb = w_in.astype(BF16)
    seg = lambda i: wb[:, :, int(offs[i]):int(offs[i + 1])]
    zpad = lambda n: jnp.zeros((DEPTH, D_MODEL, n), BF16)
    w_in_p = jnp.concatenate([
        seg(0), zpad(QK_PAD - GLA_QK), seg(1), zpad(QK_PAD - GLA_QK), seg(2), seg(3),
        seg(5), seg(6), seg(7), seg(8), jnp.repeat(seg(9), SSM_HEADDIM, axis=2),
        seg(4), zpad(LANE - GLA_RANK)], axis=2)
    assert w_in_p.shape == (DEPTH, D_MODEL, NP)
    row = lambda v: v.reshape(DEPTH, 1, -1).astype(F32)
    wgk_p = jnp.zeros((DEPTH, LANE, QK_PAD), F32).at[:, :GLA_RANK, :GLA_QK].set(w_gk).astype(BF16)
    bgk_p = jnp.zeros((DEPTH, 1, QK_PAD), F32).at[:, 0, :GLA_QK].set(b_gk)
    rep64 = lambda v: row(jnp.repeat(v, SSM_HEADDIM, axis=1))
    mixer_w = (row(norm1), w_in_p, wgk_p, bgk_p, row(jnp.tile(gla_norm, (1, GLA_HEADS))),
               conf_w.astype(F32), row(conf_b), row(conf_ln_w), row(conf_ln_b),
               ssm_conv_w.astype(F32), row(ssm_conv_b), rep64(dt_bias), rep64(a_log),
               rep64(d_skip), row(ssm_norm), w_out.astype(BF16))
    ffn_w = (row(norm2), w_ffn_in.astype(BF16), ffn_conv_w.astype(F32), row(ffn_conv_b),
             w_ffn_out.astype(BF16), final_norm.reshape(1, -1).astype(F32))
    return mixer_w, ffn_w


def _run_group(x, states, mixer_w, ffn_w, bb, T, C):
    Bn, L, _ = x.shape
    x2d = x.reshape(Bn * L, D_MODEL)
    out_states = None
    for layer in range(DEPTH):
        prev4 = None if out_states is None else out_states[:4]
        prev_ffn = None if out_states is None else out_states[4]
        x2d, ngla, nconf, nsc, nssm = _mixer_call(
            x2d, None if states is None else states[:4], prev4, mixer_w, layer, Bn, L, bb, T, C)
        x2d, nffn = _ffn_call(x2d, None if states is None else states[4], prev_ffn, ffn_w, layer, Bn, L, bb, T,
                              final=(layer == DEPTH - 1))
        out_states = (ngla, nconf, nsc, nssm, nffn)
    return x2d.reshape(Bn, L, D_MODEL), out_states


def _tiles(Lp, Bs, Ls):
    tp = min(512, Lp)
    return tp, math.gcd(tp, GLA_CHUNK), min(16, Bs), math.gcd(Ls, GLA_CHUNK)


def kernel(x_prompt, x_sample, state_gla, state_conf_conv, state_ssm_conv, state_ssm, state_ffn_conv, norm1, w_in, w_gk, b_gk, gla_norm, conf_w, conf_b, conf_ln_w, conf_ln_b, ssm_conv_w, ssm_conv_b, dt_bias, a_log, d_skip, ssm_norm, w_out, norm2, w_ffn_in, ffn_conv_w, ffn_conv_b, w_ffn_out, final_norm):
    mixer_w, ffn_w = _prep_weights(norm1, w_in, w_gk, b_gk, gla_norm, conf_w, conf_b, conf_ln_w, conf_ln_b,
                                   ssm_conv_w, ssm_conv_b, dt_bias, a_log, d_skip, ssm_norm, w_out,
                                   norm2, w_ffn_in, ffn_conv_w, ffn_conv_b, w_ffn_out, final_norm)
    tp, cp, bbs, cs = _tiles(x_prompt.shape[1], x_sample.shape[0], x_sample.shape[1])
    y_p, st_p = _run_group(x_prompt, None, mixer_w, ffn_w, 1, tp, cp)
    s_states = (state_gla, state_conf_conv, state_ssm_conv, state_ssm, state_ffn_conv)
    y_s, st_s = _run_group(x_sample, s_states, mixer_w, ffn_w, bbs, x_sample.shape[1], cs)
    return (y_p, y_s) + tuple(st_p) + tuple(st_s)
```

```python
import collections
import functools
import math

import numpy as np
import jax
import jax.numpy as jnp
from jax import lax
from jax.experimental import pallas as pl
from jax.experimental.pallas import tpu as pltpu

F32 = jnp.float32
BF16 = jnp.bfloat16

D_MODEL = 1024
DEPTH = 2
GLA_HEADS = 6
GLA_DK = 32
GLA_DV = 64
GLA_QK = GLA_HEADS * GLA_DK
GLA_WIDTH = GLA_HEADS * GLA_DV
GLA_RANK = 16
GLA_GATE_NORM = 16.0
GLA_CHUNK = 64
CONF_CH = 256
CONF_K = 31
LN_EPS = 1e-5
SSM_WIDTH = 384
SSM_HEADDIM = 64
SSM_HEADS = 6
SSM_GROUPS = 2
SSM_HPG = 3
SSM_STATE = 64
SSM_GN = SSM_GROUPS * SSM_STATE
SSM_XBC = SSM_WIDTH + 2 * SSM_GN
SSM_CONV = 4
SSM_CHUNK = 64
D_FF = 2816
FFN_CONV = 3
EPS = 1e-6
IN_SIZES = (GLA_QK, GLA_QK, GLA_WIDTH, GLA_WIDTH, GLA_RANK, CONF_CH, CONF_CH, SSM_WIDTH, SSM_XBC, SSM_HEADS)

LANE = 128
SUBLANE = 8
MXU_DIM = 256
VMEM_LIMIT_BYTES = 56 * 1024 * 1024

QK_PAD = 256
Q0 = 0
K0 = Q0 + QK_PAD
V0 = K0 + QK_PAD
GO0 = V0 + GLA_WIDTH
CA0 = GO0 + GLA_WIDTH
CB0 = CA0 + CONF_CH
Z0 = CB0 + CONF_CH
XBC0 = Z0 + SSM_WIDTH
DT0 = XBC0 + SSM_XBC
GLR0 = DT0 + SSM_WIDTH
NP = GLR0 + LANE

SPLIT_PARTS = 2

FFN_SPLITS = (768, 768, 768, 512)

CONF_BASE = 32
SSMC_BASE = 8
FFNC_BASE = 8

CHUNK_STAGE_LAG = 3
PROJ_PIECES_PER_TICK = 2


def _dot(a, b):
    return jnp.dot(a, b, preferred_element_type=F32)


def _dot_nt(a, b):
    return lax.dot_general(a, b, (((1,), (1,)), ((), ())), preferred_element_type=F32)


def _dot_tn(a, b):
    return lax.dot_general(a, b, (((0,), (0,)), ((), ())), preferred_element_type=F32)


def _split_bf16(a, parts):
    out = []
    r = a
    for i in range(parts):
        p = r.astype(BF16)
        out.append(p)
        if i + 1 < parts:
            r = r - p.astype(F32)
    return out


def _sigmoid(x):
    return 1.0 / (1.0 + jnp.exp(-x))


def _silu(x):
    return x * _sigmoid(x)


def _softplus(x):
    return jnp.maximum(x, 0.0) + jnp.log(1.0 + jnp.exp(-jnp.abs(x)))


def _log_sigmoid(x):
    return jnp.minimum(x, 0.0) - jnp.log(1.0 + jnp.exp(-jnp.abs(x)))


def _rms_scale(x):
    return x * lax.rsqrt(jnp.mean(x * x, axis=-1, keepdims=True) + EPS)


def _per_head_matmul(stacked, rhs, c):
    first_head = lax.broadcasted_iota(jnp.int32, (c, LANE), 1) < 64
    cols = []
    for col in range(3):
        two = _dot(stacked[2 * col * c:(2 * col + 2) * c, :], rhs[:, col * LANE:(col + 1) * LANE])
        cols.append(jnp.where(first_head, two[0:c, :], two[c:2 * c, :]))
    return jnp.concatenate(cols, axis=1)


def _run_staged(chunk_gens, lag, fillers=None, before_start=None, after_finish=None):
    pending = list(enumerate(chunk_gens))
    active = []
    tick = 0
    while pending or active:
        if pending and tick % lag == 0:
            i, g = pending.pop(0)
            if before_start is not None:
                before_start(i)
            active.append((i, g))
        for item in list(active):
            i, g = item
            try:
                next(g)
            except StopIteration:
                active.remove(item)
                if after_finish is not None:
                    after_finish(i)
        if fillers is not None:
            next(fillers, None)
        tick += 1


MixerCfg = collections.namedtuple("MixerCfg", "C T bb nl zero_init n_alias prefetch")


def _mixer_kernel(cfg, *refs):
    C, T, bb, nl = cfg.C, cfg.T, cfg.bb, cfg.nl
    refs = list(refs)
    x_ref = refs.pop(0)
    xnext_ref = refs.pop(0) if cfg.prefetch else None
    if cfg.zero_init:
        sgla_ref = sconf_ref = ssc_ref = sssm_ref = None
    else:
        sgla_ref, sconf_ref, ssc_ref, sssm_ref = refs[:4]
        del refs[:4]
    (n1_ref, win_ref, wgk_ref, bgk_ref, gnorm_ref, cw_ref, cbias_ref, lnw_ref, lnb_ref,
     scw_ref, scb_ref, dtb_ref, alog_ref, dskip_ref, snorm_ref, wout_ref) = refs[:16]
    del refs[:16]
    (tril_ref, tril6_ref, hm6_ref, e6_ref, gm6_ref, stmask_ref, htmask_ref) = refs[:7]
    del refs[:7]
    del refs[:cfg.n_alias]
    xo_ref, ogla_ref, oconf_ref, osc_ref, ossm_ref = refs[:5]
    proj_scr, xn_scr, st_scr, ht_scr, ubuf, xbuf, mix_scr = refs[5:]

    l = pl.program_id(1)
    nchunk = T // C
    ngroups = 2 if (bb == 1 and nchunk % 2 == 0) else 1
    RG = bb * T // ngroups
    NPIECE = NP // MXU_DIM

    def in_proj_full(g):
        rows = slice(g * RG, (g + 1) * RG)
        xn = _rms_scale(x_ref[rows, :]) * n1_ref[...]
        proj_scr[rows, :] = _dot(xn.astype(BF16), win_ref[...])

    def in_proj_pieces(g, src_ref=None):
        rows = slice(g * RG, (g + 1) * RG)
        src = x_ref[rows, :] if src_ref is None else src_ref[...]
        xn_scr[...] = (_rms_scale(src) * n1_ref[...]).astype(BF16)
        for p in range(NPIECE):
            cols = slice(p * MXU_DIM, (p + 1) * MXU_DIM)
            proj_scr[rows, cols] = _dot(xn_scr[...], win_ref[:, cols])
            if (p + 1) % PROJ_PIECES_PER_TICK == 0:
                yield

    def out_proj_pieces(g):
        rows = slice(g * RG, (g + 1) * RG)
        for p in range(D_MODEL // MXU_DIM):
            cols = slice(p * MXU_DIM, (p + 1) * MXU_DIM)
            xo_ref[rows, cols] = x_ref[rows, cols] + _dot(mix_scr[rows, :].astype(BF16), wout_ref[:, cols])
            yield

    tril = tril_ref[...]
    tril6 = tril6_ref[...] > 0.5

    def cumsum_rows(a):
        acc = None
        for p in _split_bf16(a, SPLIT_PARTS):
            t = _dot(tril, p)
            acc = t if acc is None else acc + t
        return acc

    first_half = lax.broadcasted_iota(jnp.int32, (C, LANE), 1) < 64

    def head_mean(a):
        cols = []
        for col in range(3):
            blk = a[:, col * LANE:(col + 1) * LANE]
            lo = jnp.sum(jnp.where(first_half, blk, 0.0), axis=-1, keepdims=True)
            hi = jnp.sum(jnp.where(first_half, 0.0, blk), axis=-1, keepdims=True)
            cols.append(jnp.where(first_half, lo, hi))
        return jnp.concatenate(cols, axis=1) * (1.0 / 64)

    def group_mean(a):
        c0, c1, c2 = (a[:, col * LANE:(col + 1) * LANE] for col in range(3))
        g0 = jnp.sum(c0 + jnp.where(first_half, c1, 0.0), axis=-1, keepdims=True)
        g1 = jnp.sum(c2 + jnp.where(first_half, 0.0, c1), axis=-1, keepdims=True)
        shape = (C, LANE)
        return jnp.concatenate([jnp.broadcast_to(g0, shape), jnp.where(first_half, g0, g1),
                                jnp.broadcast_to(g1, shape)], axis=1) * (1.0 / (SSM_HPG * SSM_HEADDIM))

    def chunk_stages(r0, c0, slot):
        def P(off, width):
            return proj_scr[pl.ds(r0, C), off:off + width]

        gk_pre = _dot(P(GLR0, LANE).astype(BF16), wgk_ref[...]) + bgk_ref[...]
        xbc = P(XBC0, SSM_XBC)
        ext = jnp.concatenate([xbuf[slot, pl.ds(c0, SUBLANE), :], xbc], axis=0)
        xbuf[slot, pl.ds(SSMC_BASE + c0, C), :] = xbc
        acc = jnp.broadcast_to(scb_ref[...], (C, SSM_XBC))
        for kk in range(SSM_CONV):
            lo = SUBLANE - (SSM_CONV - 1) + kk
            acc = acc + scw_ref[kk:kk + 1, :] * ext[lo:lo + C, :]
        xcv = _silu(acc)
        xs = xcv[:, 0:SSM_WIDTH]
        bm = xcv[:, SSM_WIDTH:SSM_WIDTH + SSM_GN]
        cm = xcv[:, SSM_WIDTH + SSM_GN:SSM_XBC]
        dtf = _softplus(P(DT0, SSM_WIDTH) + dtb_ref[...])
        af = dtf * (-jnp.exp(alog_ref[...]))
        yield

        gk = _log_sigmoid(gk_pre) * (1.0 / GLA_GATE_NORM)
        acum = cumsum_rows(af)
        b = cumsum_rows(gk)
        u = P(CA0, CONF_CH) * _sigmoid(P(CB0, CONF_CH))
        b0 = CONF_BASE + c0
        uext = jnp.concatenate([ubuf[slot, 0, pl.ds(b0 - SUBLANE, SUBLANE), :], u], axis=0)
        ubuf[slot, 0, pl.ds(b0, C), :] = u
        for s in range(1, SUBLANE):
            ubuf[slot, s, pl.ds(b0 - SUBLANE, C), :] = uext[s:s + C, :]
        xdt = xs * dtf
        xdtb = xdt.astype(BF16)
        cms = (jnp.concatenate([cm] * SSM_HEADS, axis=0) * gm6_ref[...]).astype(BF16)
        cb_all = _dot_nt(cms, bm.astype(BF16))

        def conf_taps(acc, k_lo, k_hi):
            for kk in range(k_lo, k_hi):
                s = (kk - (CONF_K - 1)) % SUBLANE
                off = kk - (CONF_K - 1) - s
                acc = acc + cw_ref[kk:kk + 1, :] * ubuf[slot, s, pl.ds(b0 + off, C), :]
            return acc
        yield

        alast = acum[C - 1:C, :]
        if C == SSM_CHUNK:
            acum_t = acum.T
            row_all = jnp.concatenate(
                [jnp.broadcast_to(acum_t[h * 64:h * 64 + 1, :], (C, C)) for h in range(SSM_HEADS)], axis=0)
        else:
            row_all = None
            for p in _split_bf16(acum, SPLIT_PARTS):
                t = _dot_nt(e6_ref[...], p)
                row_all = t if row_all is None else row_all + t
        blast = b[C - 1:C, :]
        q_in = P(Q0, QK_PAD) * (GLA_DK ** -0.5) * jnp.exp(b)
        k = P(K0, QK_PAD)
        k_in = k * jnp.exp(-b)
        k_out = k * jnp.exp(blast - b)
        vb = P(V0, GLA_WIDTH).astype(BF16)
        qs = (jnp.concatenate([q_in] * GLA_HEADS, axis=0) * hm6_ref[...]).astype(BF16)
        att = _dot_nt(qs, k_in.astype(BF16))
        col_all = jnp.concatenate([acum[:, h * 64:h * 64 + C] for h in range(SSM_HEADS)], axis=0)
        lmat = jnp.exp(jnp.where(tril6, col_all - row_all, -jnp.inf))
        cacc = conf_taps(jnp.broadcast_to(cbias_ref[...], (C, CONF_CH)), 0, 10)
        yield

        att = jnp.where(tril6, att, 0.0)
        o_intra = _per_head_matmul(att.astype(BF16), vb, C)
        y_intra = _per_head_matmul((cb_all * lmat).astype(BF16), xdtb, C)
        q_inb = q_in.astype(BF16)
        k_outb = k_out.astype(BF16)
        decay = jnp.exp(blast)
        o_inter = []
        for r_lo, r_hi, l_lo in ((0, 4 * GLA_DV, 0), (4 * GLA_DV, GLA_WIDTH, LANE)):
            st = st_scr[slot, r_lo:r_hi, l_lo:l_lo + LANE]
            o_inter.append(_dot_nt(q_inb[:, l_lo:l_lo + LANE], st.astype(BF16)))
            kv = _dot_tn(vb[:, r_lo:r_hi], k_outb[:, l_lo:l_lo + LANE])
            st_scr[slot, r_lo:r_hi, l_lo:l_lo + LANE] = (
                st * decay[:, l_lo:l_lo + LANE] + kv * stmask_ref[r_lo:r_hi, l_lo:l_lo + LANE])
        o = o_intra + jnp.concatenate(o_inter, axis=1)
        ht = ht_scr[slot]
        y_inter = _dot(cm.astype(BF16), ht.astype(BF16)) * jnp.exp(acum)
        cacc = conf_taps(cacc, 10, 20)
        wx = xdt * jnp.exp(alast - acum)
        stn = _dot_tn(bm.astype(BF16), wx.astype(BF16))
        ht_scr[slot] = ht * jnp.exp(alast) + stn * htmask_ref[...]
        yield

        ms_o = head_mean(o * o)
        y = y_intra + y_inter + dskip_ref[...] * xs
        y = y * _silu(P(Z0, SSM_WIDTH))
        ms_y = group_mean(y * y)
        cacc = conf_taps(cacc, 20, CONF_K)
        mu = jnp.mean(cacc, axis=-1, keepdims=True)
        xc = cacc - mu
        var = jnp.mean(xc * xc, axis=-1, keepdims=True)
        cval = _silu(xc * lax.rsqrt(var + LN_EPS) * lnw_ref[...] + lnb_ref[...])
        mix_scr[pl.ds(r0, C), GLA_WIDTH:GLA_WIDTH + CONF_CH] = cval.astype(mix_scr.dtype)
        o = o * lax.rsqrt(ms_o + EPS) * gnorm_ref[...] * _silu(P(GO0, GLA_WIDTH))
        mix_scr[pl.ds(r0, C), 0:GLA_WIDTH] = o.astype(mix_scr.dtype)
        y = y * lax.rsqrt(ms_y + EPS) * snorm_ref[...]
        mix_scr[pl.ds(r0, C), GLA_WIDTH + CONF_CH:D_MODEL] = y.astype(mix_scr.dtype)

    def load_conv_hist(slot, hconf, hsc):
        hist = jnp.concatenate([jnp.zeros((CONF_BASE - (CONF_K - 1), CONF_CH), F32), hconf], axis=0)
        ubuf[slot, 0, 0:CONF_BASE, :] = hist
        for s in range(1, SUBLANE):
            ubuf[slot, s, 0:CONF_BASE - SUBLANE, :] = hist[s:s + CONF_BASE - SUBLANE, :]
        xbuf[slot, 0:SSMC_BASE, :] = jnp.concatenate(
            [jnp.zeros((SSMC_BASE - (SSM_CONV - 1), SSM_XBC), F32), hsc], axis=0)

    def load_state(j, slot, first):
        if not first:
            load_conv_hist(slot, oconf_ref[j], osc_ref[j])
        elif cfg.zero_init:
            st_scr[slot] = jnp.zeros((GLA_WIDTH, QK_PAD), F32)
            ht_scr[slot] = jnp.zeros((SSM_GN, SSM_WIDTH), F32)
            load_conv_hist(slot, jnp.zeros((CONF_K - 1, CONF_CH), F32), jnp.zeros((SSM_CONV - 1, SSM_XBC), F32))
        else:
            rows = []
            for h in range(GLA_HEADS):
                parts = []
                if h > 0:
                    parts.append(jnp.zeros((GLA_DK, h * GLA_DV), F32))
                parts.append(sgla_ref[j, h])
                if h < GLA_HEADS - 1:
                    parts.append(jnp.zeros((GLA_DK, (GLA_HEADS - 1 - h) * GLA_DV), F32))
                rows.append(jnp.concatenate(parts, axis=1))
            rows.append(jnp.zeros((QK_PAD - GLA_QK, GLA_WIDTH), F32))
            st_scr[slot] = jnp.concatenate(rows, axis=0).T
            rows = []
            for h in range(SSM_HEADS):
                z = jnp.zeros((SSM_HEADDIM, SSM_STATE), F32)
                rows.append(jnp.concatenate([sssm_ref[j, h], z] if h // SSM_HPG == 0 else [z, sssm_ref[j, h]], axis=1))
            ht_scr[slot] = jnp.concatenate(rows, axis=0).T
            load_conv_hist(slot, sconf_ref[j], ssc_ref[j])

    def store_conv_state(j, slot):
        oconf_ref[j] = ubuf[slot, 0, T + CONF_BASE - (CONF_K - 1):T + CONF_BASE, :]
        osc_ref[j] = xbuf[slot, T + SSMC_BASE - (SSM_CONV - 1):T + SSMC_BASE, :]

    def store_state(j, slot):
        stt = st_scr[slot].T
        for h in range(GLA_HEADS):
            ogla_ref[j, h] = stt[h * GLA_DK:(h + 1) * GLA_DK, h * GLA_DV:(h + 1) * GLA_DV]
        htt = ht_scr[slot].T
        for h in range(SSM_HEADS):
            g = h // SSM_HPG
            ossm_ref[j, h] = htt[h * SSM_HEADDIM:(h + 1) * SSM_HEADDIM, g * SSM_STATE:(g + 1) * SSM_STATE]

    if bb == 1:
        if cfg.prefetch:
            @pl.when((pl.program_id(0) == 0) & (l == 0))
            def _():
                in_proj_full(0)
        else:
            in_proj_full(0)
        if nl == 1:
            load_state(0, 0, True)
        else:
            @pl.when(l == 0)
            def _():
                load_state(0, 0, True)

            @pl.when(l > 0)
            def _():
                load_state(0, 0, False)

        cpg = nchunk // ngroups
        work = collections.deque()
        if ngroups == 2:
            work.append(in_proj_pieces(1))

        def drain_work():
            while work:
                for _ in work.popleft():
                    pass

        def before_start(i):
            if ngroups == 2 and i == cpg:
                drain_work()

        def after_finish(i):
            if (i + 1) % cpg == 0:
                work.append(out_proj_pieces(i // cpg))
            if cfg.prefetch and i == cpg - 1:
                work.append(in_proj_pieces(0, xnext_ref))

        def work_pieces():
            while True:
                if work:
                    try:
                        next(work[0])
                    except StopIteration:
                        work.popleft()
                yield

        _run_staged([chunk_stages(ci * C, ci * C, 0) for ci in range(nchunk)], CHUNK_STAGE_LAG,
                    fillers=work_pieces(), before_start=before_start, after_finish=after_finish)
        drain_work()

        store_conv_state(0, 0)
        if nl == 1:
            store_state(0, 0)
        else:
            @pl.when(l == nl - 1)
            def _():
                store_state(0, 0)
    else:
        assert nl == 1 and nchunk == 1
        in_proj_full(0)

        def seq_stages(j):
            slot = j % 2
            load_state(j, slot, True)
            yield
            yield from chunk_stages(j * T, 0, slot)
            store_conv_state(j, slot)
            store_state(j, slot)

        _run_staged([seq_stages(j) for j in range(bb)], 3)
        xo_ref[...] = x_ref[...] + _dot(mix_scr[...].astype(BF16), wout_ref[...])


def _np_consts(C):
    i = np.arange(C)
    tril = (i[None, :] <= i[:, None]).astype(np.float32)
    tril6 = np.tile(tril, (6, 1))
    hm6 = np.zeros((6 * C, QK_PAD), np.float32)
    e6 = np.zeros((6 * C, GLA_WIDTH), np.float32)
    gm6 = np.zeros((6 * C, SSM_GN), np.float32)
    for h in range(6):
        hm6[h * C:(h + 1) * C, h * GLA_DK:(h + 1) * GLA_DK] = 1.0
        e6[h * C:(h + 1) * C, h * 64] = 1.0
        g = h // SSM_HPG
        gm6[h * C:(h + 1) * C, g * SSM_STATE:(g + 1) * SSM_STATE] = 1.0
    stmask = np.zeros((GLA_WIDTH, QK_PAD), np.float32)
    htmask = np.zeros((SSM_GN, SSM_WIDTH), np.float32)
    for h in range(6):
        stmask[h * 64:(h + 1) * 64, h * 32:(h + 1) * 32] = 1.0
        g = h // SSM_HPG
        htmask[g * 64:(g + 1) * 64, h * 64:(h + 1) * 64] = 1.0
    return (jnp.asarray(tril, BF16), jnp.asarray(tril6, F32), jnp.asarray(hm6, F32), jnp.asarray(e6, BF16),
            jnp.asarray(gm6, F32), jnp.asarray(stmask, F32), jnp.asarray(htmask, F32))


def _const_spec(a):
    nd = a.ndim
    return pl.BlockSpec(a.shape, lambda i, l: (0,) * nd, pipeline_mode=pl.Buffered(1))


def _layer_spec(a, layer, block=None, col=0):
    shape = a.shape[1:] if block is None else block
    return pl.BlockSpec((None,) + tuple(shape), lambda i, l: (layer, 0, col), pipeline_mode=pl.Buffered(1))


def _state_spec(a, layer, bb):
    nrest = a.ndim - 2
    return pl.BlockSpec((None, bb) + tuple(a.shape[2:]), lambda i, l: (layer, i) + (0,) * nrest)


def _mixer_call(x2d, states_in, states_prev, wts, layer, Bn, L, bb, T, C):
    assert L % T == 0 and T % C == 0 and Bn % bb == 0 and C % SUBLANE == 0
    nl = L // T
    assert bb == 1 or nl == 1
    nb = Bn // bb
    R = bb * T
    nslot = 1 if bb == 1 else 2
    consts = _np_consts(C)
    x_spec = pl.BlockSpec((R, D_MODEL), lambda i, l: (i * nl + l, 0))
    state_shapes = ((DEPTH, Bn, GLA_HEADS, GLA_DK, GLA_DV), (DEPTH, Bn, CONF_K - 1, CONF_CH),
                    (DEPTH, Bn, SSM_CONV - 1, SSM_XBC), (DEPTH, Bn, SSM_HEADS, SSM_HEADDIM, SSM_STATE))
    out_shape = ([jax.ShapeDtypeStruct((Bn * L, D_MODEL), F32)]
                 + [jax.ShapeDtypeStruct(s, F32) for s in state_shapes])
    out_specs = [x_spec] + [_state_spec(s, layer, bb) for s in out_shape[1:]]

    ngroups = 2 if (bb == 1 and (T // C) % 2 == 0) else 1
    prefetch = ngroups == 2
    operands = [x2d]
    in_specs = [x_spec]
    if prefetch:
        rg = R // ngroups
        last_block = (Bn * L) // rg - ngroups
        operands.append(x2d)
        in_specs.append(pl.BlockSpec(
            (rg, D_MODEL), lambda i, l: (jnp.minimum((i * nl + l + 1) * ngroups, last_block), 0)))
    if states_in is not None:
        operands += list(states_in)
        in_specs += [_state_spec(a, layer, bb) for a in states_in]
    operands += list(wts)
    in_specs += [_layer_spec(a, layer) for a in wts]
    operands += list(consts)
    in_specs += [_const_spec(a) for a in consts]
    aliases = {}
    if states_prev is not None:
        for k, a in enumerate(states_prev):
            aliases[len(operands)] = 1 + k
            operands.append(a)
            in_specs.append(pl.BlockSpec(memory_space=pl.ANY))
    cfg = MixerCfg(C, T, bb, nl, states_in is None, len(aliases), prefetch)
    scratch = [
        pltpu.VMEM((R, NP), F32),
        pltpu.VMEM((R // ngroups, D_MODEL), BF16),
        pltpu.VMEM((nslot, GLA_WIDTH, QK_PAD), F32),
        pltpu.VMEM((nslot, SSM_GN, SSM_WIDTH), F32),
        pltpu.VMEM((nslot, SUBLANE, CONF_BASE + T, CONF_CH), F32),
        pltpu.VMEM((nslot, SSMC_BASE + T, SSM_XBC), F32),
        pltpu.VMEM((R, D_MODEL), BF16 if C % (2 * SUBLANE) == 0 else F32),
    ]
    return pl.pallas_call(
        functools.partial(_mixer_kernel, cfg),
        grid=(nb, nl),
        in_specs=in_specs,
        out_specs=out_specs,
        out_shape=out_shape,
        scratch_shapes=scratch,
        input_output_aliases=aliases,
        compiler_params=pltpu.CompilerParams(
            dimension_semantics=("arbitrary", "arbitrary"), vmem_limit_bytes=VMEM_LIMIT_BYTES),
        name="mixer",
    )(*operands)


FfnCfg = collections.namedtuple("FfnCfg", "T bb nl final zero_init n_alias")


def _ffn_kernel(cfg, *refs):
    T, bb, nl = cfg.T, cfg.bb, cfg.nl
    refs = list(refs)
    x_ref = refs.pop(0)
    sffn_ref = None if cfg.zero_init else refs.pop(0)
    n2_ref, wa_ref, wg_ref, fcw_ref, fcb_ref, wo_ref, fn_ref = refs[:7]
    del refs[:7]
    del refs[:cfg.n_alias]
    xo_ref, offn_ref, abuf, act = refs

    l = pl.program_id(1)
    R = bb * T
    x = x_ref[...]
    xn = (_rms_scale(x) * n2_ref[...]).astype(BF16)
    lo = FFNC_BASE - (FFN_CONV - 1)

    def init_hist():
        if cfg.zero_init:
            abuf[:, lo:FFNC_BASE, :] = jnp.zeros((bb, FFN_CONV - 1, D_FF), F32)
        else:
            abuf[:, lo:FFNC_BASE, :] = sffn_ref[...]

    if nl == 1:
        init_hist()
    else:
        @pl.when(l == 0)
        def _():
            init_hist()

        @pl.when(l > 0)
        def _():
            abuf[:, lo:FFNC_BASE, :] = offn_ref[...]
    f0 = 0
    for fw in FFN_SPLITS:
        a = _dot(xn, wa_ref[:, f0:f0 + fw])
        g = _dot(xn, wg_ref[:, f0:f0 + fw])
        abuf[:, FFNC_BASE:FFNC_BASE + T, f0:f0 + fw] = a.reshape(bb, T, fw)
        ac = fcb_ref[:, f0:f0 + fw][None]
        for kk in range(FFN_CONV):
            ac = ac + fcw_ref[kk:kk + 1, f0:f0 + fw][None] * abuf[:, lo + kk:lo + kk + T, f0:f0 + fw]
        offn_ref[:, :, f0:f0 + fw] = abuf[:, T + lo:T + FFNC_BASE, f0:f0 + fw]
        act[:, f0:f0 + fw] = (_silu(ac).reshape(R, fw) * g).astype(BF16)
        f0 += fw
    y = x + _dot(act[...], wo_ref[...])
    if cfg.final:
        y = _rms_scale(y) * fn_ref[...]
    xo_ref[...] = y


def _ffn_call(x2d, state_in, state_prev, wts, layer, Bn, L, bb, T, final):
    assert L % T == 0 and Bn % bb == 0
    nl = L // T
    assert bb == 1 or nl == 1
    nb = Bn // bb
    R = bb * T
    n2, w_ffn_in, fcw, fcb, w_ffn_out, fn = wts
    x_spec = pl.BlockSpec((R, D_MODEL), lambda i, l: (i * nl + l, 0))
    st_shape = jax.ShapeDtypeStruct((DEPTH, Bn, FFN_CONV - 1, D_FF), F32)
    operands = [x2d]
    in_specs = [x_spec]
    if state_in is not None:
        operands.append(state_in)
        in_specs.append(_state_spec(state_in, layer, bb))
    operands += [n2, w_ffn_in, w_ffn_in, fcw, fcb, w_ffn_out, fn]
    in_specs += [_layer_spec(n2, layer),
                 _layer_spec(w_ffn_in, layer, block=(D_MODEL, D_FF), col=0),
                 _layer_spec(w_ffn_in, layer, block=(D_MODEL, D_FF), col=1),
                 _layer_spec(fcw, layer), _layer_spec(fcb, layer), _layer_spec(w_ffn_out, layer), _const_spec(fn)]
    aliases = {}
    if state_prev is not None:
        aliases[len(operands)] = 1
        operands.append(state_prev)
        in_specs.append(pl.BlockSpec(memory_space=pl.ANY))
    cfg = FfnCfg(T, bb, nl, final, state_in is None, len(aliases))
    return pl.pallas_call(
        functools.partial(_ffn_kernel, cfg),
        grid=(nb, nl),
        in_specs=in_specs,
        out_specs=[x_spec, _state_spec(st_shape, layer, bb)],
        out_shape=[jax.ShapeDtypeStruct((Bn * L, D_MODEL), F32), st_shape],
        scratch_shapes=[pltpu.VMEM((bb, FFNC_BASE + T, D_FF), F32),
                        pltpu.VMEM((R, D_FF), BF16)],
        input_output_aliases=aliases,
        compiler_params=pltpu.CompilerParams(
            dimension_semantics=("arbitrary", "arbitrary"), vmem_limit_bytes=VMEM_LIMIT_BYTES),
        name="ffn",
    )(*operands)


def _prep_weights(norm1, w_in, w_gk, b_gk, gla_norm, conf_w, conf_b, conf_ln_w, conf_ln_b,
                  ssm_conv_w, ssm_conv_b, dt_bias, a_log, d_skip, ssm_norm, w_out,
                  norm2, w_ffn_in, ffn_conv_w, ffn_conv_b, w_ffn_out, final_norm):
    offs = np.concatenate([[0], np.cumsum(IN_SIZES)])
    wb = w_in.astype(BF16)
    seg = lambda i: wb[:, :, int(offs[i]):int(offs[i + 1])]
    zpad = lambda n: jnp.zeros((DEPTH, D_MODEL, n), BF16)
    w_in_p = jnp.concatenate([
        seg(0), zpad(QK_PAD - GLA_QK), seg(1), zpad(QK_PAD - GLA_QK), seg(2), seg(3),
        seg(5), seg(6), seg(7), seg(8), jnp.repeat(seg(9), SSM_HEADDIM, axis=2),
        seg(4), zpad(LANE - GLA_RANK)], axis=2)
    assert w_in_p.shape == (DEPTH, D_MODEL, NP)
    row = lambda v: v.reshape(DEPTH, 1, -1).astype(F32)
    wgk_p = jnp.zeros((DEPTH, LANE, QK_PAD), F32).at[:, :GLA_RANK, :GLA_QK].set(w_gk).astype(BF16)
    bgk_p = jnp.zeros((DEPTH, 1, QK_PAD), F32).at[:, 0, :GLA_QK].set(b_gk)
    rep64 = lambda v: row(jnp.repeat(v, SSM_HEADDIM, axis=1))
    mixer_w = (row(norm1), w_in_p, wgk_p, bgk_p, row(jnp.tile(gla_norm, (1, GLA_HEADS))),
               conf_w.astype(F32), row(conf_b), row(conf_ln_w), row(conf_ln_b),
               ssm_conv_w.astype(F32), row(ssm_conv_b), rep64(dt_bias), rep64(a_log),
               rep64(d_skip), row(ssm_norm), w_out.astype(BF16))
    ffn_w = (row(norm2), w_ffn_in.astype(BF16), ffn_conv_w.astype(F32), row(ffn_conv_b),
             w_ffn_out.astype(BF16), final_norm.reshape(1, -1).astype(F32))
    return mixer_w, ffn_w


def _run_group(x, states, mixer_w, ffn_w, bb, T, C):
    Bn, L, _ = x.shape
    x2d = x.reshape(Bn * L, D_MODEL)
    out_states = None
    for layer in range(DEPTH):
        prev4 = None if out_states is None else out_states[:4]
        prev_ffn = None if out_states is None else out_states[4]
        x2d, ngla, nconf, nsc, nssm = _mixer_call(
            x2d, None if states is None else states[:4], prev4, mixer_w, layer, Bn, L, bb, T, C)
        x2d, nffn = _ffn_call(x2d, None if states is None else states[4], prev_ffn, ffn_w, layer, Bn, L, bb, T,
                              final=(layer == DEPTH - 1))
        out_states = (ngla, nconf, nsc, nssm, nffn)
    return x2d.reshape(Bn, L, D_MODEL), out_states


def _tiles(Lp, Bs, Ls):
    tp = min(512, Lp)
    return tp, math.gcd(tp, GLA_CHUNK), min(16, Bs), math.gcd(Ls, GLA_CHUNK)


def kernel(x_prompt, x_sample, state_gla, state_conf_conv, state_ssm_conv, state_ssm, state_ffn_conv, norm1, w_in, w_gk, b_gk, gla_norm, conf_w, conf_b, conf_ln_w, conf_ln_b, ssm_conv_w, ssm_conv_b, dt_bias, a_log, d_skip, ssm_norm, w_out, norm2, w_ffn_in, ffn_conv_w, ffn_conv_b, w_ffn_out, final_norm):
    mixer_w, ffn_w = _prep_weights(norm1, w_in, w_gk, b_gk, gla_norm, conf_w, conf_b, conf_ln_w, conf_ln_b,
                                   ssm_conv_w, ssm_conv_b, dt_bias, a_log, d_skip, ssm_norm, w_out,
                                   norm2, w_ffn_in, ffn_conv_w, ffn_conv_b, w_ffn_out, final_norm)
    tp, cp, bbs, cs = _tiles(x_prompt.shape[1], x_sample.shape[0], x_sample.shape[1])
    y_p, st_p = _run_group(x_prompt, None, mixer_w, ffn_w, 1, tp, cp)
    s_states = (state_gla, state_conf_conv, state_ssm_conv, state_ssm, state_ffn_conv)
    y_s, st_s = _run_group(x_sample, s_states, mixer_w, ffn_w, bbs, x_sample.shape[1], cs)
    return (y_p, y_s) + tuple(st_p) + tuple(st_s)
```

```python
import collections
import functools
import math

import numpy as np
import jax
import jax.numpy as jnp
from jax import lax
from jax.experimental import pallas as pl
from jax.experimental.pallas import tpu as pltpu

F32 = jnp.float32
BF16 = jnp.bfloat16

D_MODEL = 1024
DEPTH = 2
GLA_HEADS = 6
GLA_DK = 32
GLA_DV = 64
GLA_QK = GLA_HEADS * GLA_DK
GLA_WIDTH = GLA_HEADS * GLA_DV
GLA_RANK = 16
GLA_GATE_NORM = 16.0
GLA_CHUNK = 64
CONF_CH = 256
CONF_K = 31
LN_EPS = 1e-5
SSM_WIDTH = 384
SSM_HEADDIM = 64
SSM_HEADS = 6
SSM_GROUPS = 2
SSM_HPG = 3
SSM_STATE = 64
SSM_GN = SSM_GROUPS * SSM_STATE
SSM_XBC = SSM_WIDTH + 2 * SSM_GN
SSM_CONV = 4
SSM_CHUNK = 64
D_FF = 2816
FFN_CONV = 3
EPS = 1e-6
IN_SIZES = (GLA_QK, GLA_QK, GLA_WIDTH, GLA_WIDTH, GLA_RANK, CONF_CH, CONF_CH, SSM_WIDTH, SSM_XBC, SSM_HEADS)

LANE = 128
SUBLANE = 8
MXU_DIM = 256
VMEM_LIMIT_BYTES = 56 * 1024 * 1024

QK_PAD = 256
Q0 = 0
K0 = Q0 + QK_PAD
V0 = K0 + QK_PAD
GO0 = V0 + GLA_WIDTH
CA0 = GO0 + GLA_WIDTH
CB0 = CA0 + CONF_CH
Z0 = CB0 + CONF_CH
XBC0 = Z0 + SSM_WIDTH
DT0 = XBC0 + SSM_XBC
GLR0 = DT0 + SSM_WIDTH
NP = GLR0 + LANE

SPLIT_PARTS = 2

FFN_SPLITS = (768, 768, 768, 512)

CONF_BASE = 32
SSMC_BASE = 8
FFNC_BASE = 8

CHUNK_STAGE_LAG = 3
SEQ_STAGES = 6
SEQ_STAGE_LAG = 1
SEQ_SLOTS = -(-SEQ_STAGES // SEQ_STAGE_LAG)
PROJ_PIECES_PER_TICK = 2


def _dot(a, b):
    return jnp.dot(a, b, preferred_element_type=F32)


def _dot_nt(a, b):
    return lax.dot_general(a, b, (((1,), (1,)), ((), ())), preferred_element_type=F32)


def _dot_tn(a, b):
    return lax.dot_general(a, b, (((0,), (0,)), ((), ())), preferred_element_type=F32)


def _split_bf16(a, parts):
    out = []
    r = a
    for i in range(parts):
        p = r.astype(BF16)
        out.append(p)
        if i + 1 < parts:
            r = r - p.astype(F32)
    return out


def _sigmoid(x):
    return 1.0 / (1.0 + jnp.exp(-x))


def _silu(x):
    return x * _sigmoid(x)


def _softplus(x):
    return jnp.maximum(x, 0.0) + jnp.log(1.0 + jnp.exp(-jnp.abs(x)))


def _log_sigmoid(x):
    return jnp.minimum(x, 0.0) - jnp.log(1.0 + jnp.exp(-jnp.abs(x)))


def _rms_scale(x):
    return x * lax.rsqrt(jnp.mean(x * x, axis=-1, keepdims=True) + EPS)


def _per_head_matmul(cat, rhs, c):
    first_head = lax.broadcasted_iota(jnp.int32, (c, LANE), 1) < 64
    packed = c % (2 * SUBLANE) == 0
    if packed:
        keep_lo = jnp.where(first_head, 1.0, 0.0).astype(BF16)
        keep_hi = jnp.where(first_head, 0.0, 1.0).astype(BF16)
    cols = []
    for col in range(3):
        r = rhs[:, col * LANE:(col + 1) * LANE]
        if packed:
            rb = r.astype(BF16)
            two = jnp.concatenate([rb * keep_lo, rb * keep_hi], axis=0)
        else:
            two = jnp.concatenate([jnp.where(first_head, r, 0.0), jnp.where(first_head, 0.0, r)],
                                  axis=0).astype(BF16)
        cols.append(_dot(cat[:, 2 * col * c:(2 * col + 2) * c], two))
    return jnp.concatenate(cols, axis=1)


def _run_staged(chunk_gens, lag, fillers=None, before_start=None, after_finish=None):
    pending = list(enumerate(chunk_gens))
    active = []
    tick = 0
    while pending or active:
        if pending and tick % lag == 0:
            i, g = pending.pop(0)
            if before_start is not None:
                before_start(i)
            active.append((i, g))
        for item in list(active):
            i, g = item
            try:
                next(g)
            except StopIteration:
                active.remove(item)
                if after_finish is not None:
                    after_finish(i)
        if fillers is not None:
            next(fillers, None)
        tick += 1


MixerCfg = collections.namedtuple("MixerCfg", "C T bb nl zero_init n_alias prefetch")


def _mixer_kernel(cfg, *refs):
    C, T, bb, nl = cfg.C, cfg.T, cfg.bb, cfg.nl
    refs = list(refs)
    x_ref = refs.pop(0)
    xnext_ref = refs.pop(0) if cfg.prefetch else None
    if cfg.zero_init:
        sgla_ref = sconf_ref = ssc_ref = sssm_ref = None
    else:
        sgla_ref, sconf_ref, ssc_ref, sssm_ref = refs[:4]
        del refs[:4]
    (n1_ref, win_ref, wgk_ref, bgk_ref, gnorm_ref, cw_ref, cbias_ref, lnw_ref, lnb_ref,
     scw_ref, scb_ref, dtb_ref, alog_ref, dskip_ref, snorm_ref, wout_ref) = refs[:16]
    del refs[:16]
    (tril_ref, trilcat_ref, hm6_ref, eyecat_ref, gm6_ref, stmask_ref, htmask_ref) = refs[:7]
    del refs[:7]
    del refs[:cfg.n_alias]
    xo_ref, ogla_ref, oconf_ref, osc_ref, ossm_ref = refs[:5]
    proj_scr, xn_scr, st_scr, ht_scr, ubuf, xbuf, mix_scr = refs[5:]

    l = pl.program_id(1)
    nchunk = T // C
    ngroups = 2 if (bb == 1 and nchunk % 2 == 0) else 1
    RG = bb * T // ngroups
    NPIECE = NP // MXU_DIM

    def in_proj_full(g):
        rows = slice(g * RG, (g + 1) * RG)
        xn = _rms_scale(x_ref[rows, :]) * n1_ref[...]
        proj_scr[rows, :] = _dot(xn.astype(BF16), win_ref[...])

    def in_proj_pieces(g, src_ref=None):
        rows = slice(g * RG, (g + 1) * RG)
        src = x_ref[rows, :] if src_ref is None else src_ref[...]
        xn_scr[...] = (_rms_scale(src) * n1_ref[...]).astype(BF16)
        for p in range(NPIECE):
            cols = slice(p * MXU_DIM, (p + 1) * MXU_DIM)
            proj_scr[rows, cols] = _dot(xn_scr[...], win_ref[:, cols])
            if (p + 1) % PROJ_PIECES_PER_TICK == 0:
                yield

    def out_proj_pieces(g):
        rows = slice(g * RG, (g + 1) * RG)
        for p in range(D_MODEL // MXU_DIM):
            cols = slice(p * MXU_DIM, (p + 1) * MXU_DIM)
            xo_ref[rows, cols] = x_ref[rows, cols] + _dot(mix_scr[rows, :].astype(BF16), wout_ref[:, cols])
            yield

    tril = tril_ref[...]
    trilcat = trilcat_ref[...] > 0.5

    def cumsum_rows(a):
        acc = None
        for p in _split_bf16(a, SPLIT_PARTS):
            t = _dot(tril, p)
            acc = t if acc is None else acc + t
        return acc

    first_half = lax.broadcasted_iota(jnp.int32, (C, LANE), 1) < 64

    def stack_masked(a, mask_ref, n):
        if C % (2 * SUBLANE) == 0:
            return jnp.concatenate([a.astype(BF16)] * n, axis=0) * mask_ref[...]
        return (jnp.concatenate([a] * n, axis=0) * mask_ref[...].astype(F32)).astype(BF16)

    def head_mean(a):
        cols = []
        for col in range(3):
            blk = a[:, col * LANE:(col + 1) * LANE]
            lo = jnp.sum(jnp.where(first_half, blk, 0.0), axis=-1, keepdims=True)
            hi = jnp.sum(jnp.where(first_half, 0.0, blk), axis=-1, keepdims=True)
            cols.append(jnp.where(first_half, lo, hi))
        return jnp.concatenate(cols, axis=1) * (1.0 / 64)

    def group_mean(a):
        c0, c1, c2 = (a[:, col * LANE:(col + 1) * LANE] for col in range(3))
        g0 = jnp.sum(c0 + jnp.where(first_half, c1, 0.0), axis=-1, keepdims=True)
        g1 = jnp.sum(c2 + jnp.where(first_half, 0.0, c1), axis=-1, keepdims=True)
        shape = (C, LANE)
        return jnp.concatenate([jnp.broadcast_to(g0, shape), jnp.where(first_half, g0, g1),
                                jnp.broadcast_to(g1, shape)], axis=1) * (1.0 / (SSM_HPG * SSM_HEADDIM))

    def chunk_stages(r0, c0, slot):
        def P(off, width):
            return proj_scr[pl.ds(r0, C), off:off + width]

        gk_pre = _dot(P(GLR0, LANE).astype(BF16), wgk_ref[...]) + bgk_ref[...]
        xbc = P(XBC0, SSM_XBC)
        ext = jnp.concatenate([xbuf[slot, pl.ds(c0, SUBLANE), :], xbc], axis=0)
        xbuf[slot, pl.ds(SSMC_BASE + c0, C), :] = xbc
        acc = jnp.broadcast_to(scb_ref[...], (C, SSM_XBC))
        for kk in range(SSM_CONV):
            lo = SUBLANE - (SSM_CONV - 1) + kk
            acc = acc + scw_ref[kk:kk + 1, :] * ext[lo:lo + C, :]
        xcv = _silu(acc)
        xs = xcv[:, 0:SSM_WIDTH]
        bm = xcv[:, SSM_WIDTH:SSM_WIDTH + SSM_GN]
        cm = xcv[:, SSM_WIDTH + SSM_GN:SSM_XBC]
        dtf = _softplus(P(DT0, SSM_WIDTH) + dtb_ref[...])
        af = dtf * (-jnp.exp(alog_ref[...]))
        yield

        gk = _log_sigmoid(gk_pre) * (1.0 / GLA_GATE_NORM)
        acum = cumsum_rows(af)
        b = cumsum_rows(gk)
        u = P(CA0, CONF_CH) * _sigmoid(P(CB0, CONF_CH))
        b0 = CONF_BASE + c0
        uext = jnp.concatenate([ubuf[slot, 0, pl.ds(b0 - SUBLANE, SUBLANE), :], u], axis=0)
        ubuf[slot, 0, pl.ds(b0, C), :] = u
        for s in range(1, SUBLANE):
            ubuf[slot, s, pl.ds(b0 - SUBLANE, C), :] = uext[s:s + C, :]
        xdt = xs * dtf
        bbd = stack_masked(bm, gm6_ref, SSM_HEADS)
        cb_cat = _dot_nt(cm.astype(BF16), bbd)

        def conf_taps(acc, k_lo, k_hi):
            for kk in range(k_lo, k_hi):
                s = (kk - (CONF_K - 1)) % SUBLANE
                off = kk - (CONF_K - 1) - s
                acc = acc + cw_ref[kk:kk + 1, :] * ubuf[slot, s, pl.ds(b0 + off, C), :]
            return acc
        yield

        alast = acum[C - 1:C, :]
        if C == SSM_CHUNK:
            col_cat = acum
            acum_t = acum.T
            row_cat = jnp.concatenate([acum_t[h * 64:h * 64 + 1, :] for h in range(SSM_HEADS)], axis=1)
        else:
            col_cat = jnp.concatenate([acum[:, h * 64:h * 64 + C] for h in range(SSM_HEADS)], axis=1)
            ones_cc = jnp.ones((C, C), BF16)
            row_cat = None
            for p in _split_bf16(col_cat * eyecat_ref[...], SPLIT_PARTS):
                t = _dot(ones_cc, p)
                row_cat = t if row_cat is None else row_cat + t
        lmat = jnp.exp(jnp.where(trilcat, col_cat - row_cat, -jnp.inf))
        blast = b[C - 1:C, :]
        q_in = P(Q0, QK_PAD) * (GLA_DK ** -0.5) * jnp.exp(b)
        k = P(K0, QK_PAD)
        k_in = k * jnp.exp(-b)
        k_out = k * jnp.exp(blast - b)
        v = P(V0, GLA_WIDTH)
        vb = v.astype(BF16)
        q_inb = q_in.astype(BF16)
        kbd = stack_masked(k_in, hm6_ref, GLA_HEADS)
        att = _dot_nt(q_inb, kbd)
        cacc = conf_taps(jnp.broadcast_to(cbias_ref[...], (C, CONF_CH)), 0, 10)
        yield

        att = jnp.where(trilcat, att, 0.0)
        o_intra = _per_head_matmul(att.astype(BF16), v, C)
        y_intra = _per_head_matmul((cb_cat * lmat).astype(BF16), xdt, C)
        k_outb = k_out.astype(BF16)
        decay = jnp.exp(blast)
        o_inter = []
        for r_lo, r_hi, l_lo in ((0, 4 * GLA_DV, 0), (4 * GLA_DV, GLA_WIDTH, LANE)):
            st = st_scr[slot, r_lo:r_hi, l_lo:l_lo + LANE]
            o_inter.append(_dot_nt(q_inb[:, l_lo:l_lo + LANE], st.astype(BF16)))
            kv = _dot_tn(vb[:, r_lo:r_hi], k_outb[:, l_lo:l_lo + LANE])
            st_scr[slot, r_lo:r_hi, l_lo:l_lo + LANE] = (
                st * decay[:, l_lo:l_lo + LANE] + kv * stmask_ref[r_lo:r_hi, l_lo:l_lo + LANE])
        o = o_intra + jnp.concatenate(o_inter, axis=1)
        ht = ht_scr[slot]
        y_inter = _dot(cm.astype(BF16), ht.astype(BF16)) * jnp.exp(acum)
        cacc = conf_taps(cacc, 10, 20)
        wx = xdt * jnp.exp(alast - acum)
        stn = _dot_tn(bm.astype(BF16), wx.astype(BF16))
        ht_scr[slot] = ht * jnp.exp(alast) + stn * htmask_ref[...]
        yield

        ms_o = head_mean(o * o)
        y = y_intra + y_inter + dskip_ref[...] * xs
        y = y * _silu(P(Z0, SSM_WIDTH))
        ms_y = group_mean(y * y)
        cacc = conf_taps(cacc, 20, CONF_K)
        mu = jnp.mean(cacc, axis=-1, keepdims=True)
        xc = cacc - mu
        var = jnp.mean(xc * xc, axis=-1, keepdims=True)
        cval = _silu(xc * lax.rsqrt(var + LN_EPS) * lnw_ref[...] + lnb_ref[...])
        mix_scr[pl.ds(r0, C), GLA_WIDTH:GLA_WIDTH + CONF_CH] = cval.astype(mix_scr.dtype)
        o = o * lax.rsqrt(ms_o + EPS) * gnorm_ref[...] * _silu(P(GO0, GLA_WIDTH))
        mix_scr[pl.ds(r0, C), 0:GLA_WIDTH] = o.astype(mix_scr.dtype)
        y = y * lax.rsqrt(ms_y + EPS) * snorm_ref[...]
        mix_scr[pl.ds(r0, C), GLA_WIDTH + CONF_CH:D_MODEL] = y.astype(mix_scr.dtype)

    def load_conv_hist(slot, hconf, hsc):
        hist = jnp.concatenate([jnp.zeros((CONF_BASE - (CONF_K - 1), CONF_CH), F32), hconf], axis=0)
        ubuf[slot, 0, 0:CONF_BASE, :] = hist
        for s in range(1, SUBLANE):
            ubuf[slot, s, 0:CONF_BASE - SUBLANE, :] = hist[s:s + CONF_BASE - SUBLANE, :]
        xbuf[slot, 0:SSMC_BASE, :] = jnp.concatenate(
            [jnp.zeros((SSMC_BASE - (SSM_CONV - 1), SSM_XBC), F32), hsc], axis=0)

    def load_state(j, slot, first):
        if not first:
            load_conv_hist(slot, oconf_ref[j], osc_ref[j])
        elif cfg.zero_init:
            st_scr[slot] = jnp.zeros((GLA_WIDTH, QK_PAD), F32)
            ht_scr[slot] = jnp.zeros((SSM_GN, SSM_WIDTH), F32)
            load_conv_hist(slot, jnp.zeros((CONF_K - 1, CONF_CH), F32), jnp.zeros((SSM_CONV - 1, SSM_XBC), F32))
        else:
            rows = []
            for h in range(GLA_HEADS):
                parts = []
                if h > 0:
                    parts.append(jnp.zeros((GLA_DK, h * GLA_DV), F32))
                parts.append(sgla_ref[j, h])
                if h < GLA_HEADS - 1:
                    parts.append(jnp.zeros((GLA_DK, (GLA_HEADS - 1 - h) * GLA_DV), F32))
                rows.append(jnp.concatenate(parts, axis=1))
            rows.append(jnp.zeros((QK_PAD - GLA_QK, GLA_WIDTH), F32))
            st_scr[slot] = jnp.concatenate(rows, axis=0).T
            rows = []
            for h in range(SSM_HEADS):
                z = jnp.zeros((SSM_HEADDIM, SSM_STATE), F32)
                rows.append(jnp.concatenate([sssm_ref[j, h], z] if h // SSM_HPG == 0 else [z, sssm_ref[j, h]], axis=1))
            ht_scr[slot] = jnp.concatenate(rows, axis=0).T
            load_conv_hist(slot, sconf_ref[j], ssc_ref[j])

    def store_conv_state(j, slot):
        oconf_ref[j] = ubuf[slot, 0, T + CONF_BASE - (CONF_K - 1):T + CONF_BASE, :]
        osc_ref[j] = xbuf[slot, T + SSMC_BASE - (SSM_CONV - 1):T + SSMC_BASE, :]

    def store_state(j, slot):
        stt = st_scr[slot].T
        for h in range(GLA_HEADS):
            ogla_ref[j, h] = stt[h * GLA_DK:(h + 1) * GLA_DK, h * GLA_DV:(h + 1) * GLA_DV]
        htt = ht_scr[slot].T
        for h in range(SSM_HEADS):
            g = h // SSM_HPG
            ossm_ref[j, h] = htt[h * SSM_HEADDIM:(h + 1) * SSM_HEADDIM, g * SSM_STATE:(g + 1) * SSM_STATE]

    if bb == 1:
        if cfg.prefetch:
            @pl.when((pl.program_id(0) == 0) & (l == 0))
            def _():
                in_proj_full(0)
        else:
            in_proj_full(0)
        if nl == 1:
            load_state(0, 0, True)
        else:
            @pl.when(l == 0)
            def _():
                load_state(0, 0, True)

            @pl.when(l > 0)
            def _():
                load_state(0, 0, False)

        cpg = nchunk // ngroups
        work = collections.deque()
        if ngroups == 2:
            work.append(in_proj_pieces(1))

        def drain_work():
            while work:
                for _ in work.popleft():
                    pass

        def before_start(i):
            if ngroups == 2 and i == cpg:
                drain_work()

        def after_finish(i):
            if (i + 1) % cpg == 0:
                work.append(out_proj_pieces(i // cpg))
            if cfg.prefetch and i == cpg - 1:
                work.append(in_proj_pieces(0, xnext_ref))

        def work_pieces():
            while True:
                if work:
                    try:
                        next(work[0])
                    except StopIteration:
                        work.popleft()
                yield

        _run_staged([chunk_stages(ci * C, ci * C, 0) for ci in range(nchunk)], CHUNK_STAGE_LAG,
                    fillers=work_pieces(), before_start=before_start, after_finish=after_finish)
        drain_work()

        store_conv_state(0, 0)
        if nl == 1:
            store_state(0, 0)
        else:
            @pl.when(l == nl - 1)
            def _():
                store_state(0, 0)
    else:
        assert nl == 1 and nchunk == 1
        in_proj_full(0)

        def seq_stages(j):
            slot = j % SEQ_SLOTS
            load_state(j, slot, True)
            yield
            yield from chunk_stages(j * T, 0, slot)
            store_conv_state(j, slot)
            store_state(j, slot)

        _run_staged([seq_stages(j) for j in range(bb)], SEQ_STAGE_LAG)
        xo_ref[...] = x_ref[...] + _dot(mix_scr[...].astype(BF16), wout_ref[...])


def _np_consts(C):
    i = np.arange(C)
    tril = (i[None, :] <= i[:, None]).astype(np.float32)
    trilcat = np.tile(tril, (1, 6))
    eyecat = np.tile(np.eye(C, dtype=np.float32), (1, 6))
    hm6 = np.zeros((6 * C, QK_PAD), np.float32)
    gm6 = np.zeros((6 * C, SSM_GN), np.float32)
    for h in range(6):
        hm6[h * C:(h + 1) * C, h * GLA_DK:(h + 1) * GLA_DK] = 1.0
        g = h // SSM_HPG
        gm6[h * C:(h + 1) * C, g * SSM_STATE:(g + 1) * SSM_STATE] = 1.0
    stmask = np.zeros((GLA_WIDTH, QK_PAD), np.float32)
    htmask = np.zeros((SSM_GN, SSM_WIDTH), np.float32)
    for h in range(6):
        stmask[h * 64:(h + 1) * 64, h * 32:(h + 1) * 32] = 1.0
        g = h // SSM_HPG
        htmask[g * 64:(g + 1) * 64, h * 64:(h + 1) * 64] = 1.0
    return (jnp.asarray(tril, BF16), jnp.asarray(trilcat, F32), jnp.asarray(hm6, BF16), jnp.asarray(eyecat, F32),
            jnp.asarray(gm6, BF16), jnp.asarray(stmask, F32), jnp.asarray(htmask, F32))


def _const_spec(a):
    nd = a.ndim
    return pl.BlockSpec(a.shape, lambda i, l: (0,) * nd, pipeline_mode=pl.Buffered(1))


def _layer_spec(a, layer, block=None, col=0):
    shape = a.shape[1:] if block is None else block
    return pl.BlockSpec((None,) + tuple(shape), lambda i, l: (layer, 0, col), pipeline_mode=pl.Buffered(1))


def _state_spec(a, layer, bb):
    nrest = a.ndim - 2
    return pl.BlockSpec((None, bb) + tuple(a.shape[2:]), lambda i, l: (layer, i) + (0,) * nrest)


def _mixer_call(x2d, states_in, states_prev, wts, layer, Bn, L, bb, T, C):
    assert L % T == 0 and T % C == 0 and Bn % bb == 0 and C % SUBLANE == 0
    nl = L // T
    assert bb == 1 or nl == 1
    nb = Bn // bb
    R = bb * T
    nslot = 1 if bb == 1 else SEQ_SLOTS
    consts = _np_consts(C)
    x_spec = pl.BlockSpec((R, D_MODEL), lambda i, l: (i * nl + l, 0))
    state_shapes = ((DEPTH, Bn, GLA_HEADS, GLA_DK, GLA_DV), (DEPTH, Bn, CONF_K - 1, CONF_CH),
                    (DEPTH, Bn, SSM_CONV - 1, SSM_XBC), (DEPTH, Bn, SSM_HEADS, SSM_HEADDIM, SSM_STATE))
    out_shape = ([jax.ShapeDtypeStruct((Bn * L, D_MODEL), F32)]
                 + [jax.ShapeDtypeStruct(s, F32) for s in state_shapes])
    out_specs = [x_spec] + [_state_spec(s, layer, bb) for s in out_shape[1:]]

    ngroups = 2 if (bb == 1 and (T // C) % 2 == 0) else 1
    prefetch = ngroups == 2
    operands = [x2d]
    in_specs = [x_spec]
    if prefetch:
        rg = R // ngroups
        last_block = (Bn * L) // rg - ngroups
        operands.append(x2d)
        in_specs.append(pl.BlockSpec(
            (rg, D_MODEL), lambda i, l: (jnp.minimum((i * nl + l + 1) * ngroups, last_block), 0)))
    if states_in is not None:
        operands += list(states_in)
        in_specs += [_state_spec(a, layer, bb) for a in states_in]
    operands += list(wts)
    in_specs += [_layer_spec(a, layer) for a in wts]
    operands += list(consts)
    in_specs += [_const_spec(a) for a in consts]
    aliases = {}
    if states_prev is not None:
        for k, a in enumerate(states_prev):
            aliases[len(operands)] = 1 + k
            operands.append(a)
            in_specs.append(pl.BlockSpec(memory_space=pl.ANY))
    cfg = MixerCfg(C, T, bb, nl, states_in is None, len(aliases), prefetch)
    scratch = [
        pltpu.VMEM((R, NP), F32),
        pltpu.VMEM((R // ngroups, D_MODEL), BF16),
        pltpu.VMEM((nslot, GLA_WIDTH, QK_PAD), F32),
        pltpu.VMEM((nslot, SSM_GN, SSM_WIDTH), F32),
        pltpu.VMEM((nslot, SUBLANE, CONF_BASE + T, CONF_CH), F32),
        pltpu.VMEM((nslot, SSMC_BASE + T, SSM_XBC), F32),
        pltpu.VMEM((R, D_MODEL), BF16 if C % (2 * SUBLANE) == 0 else F32),
    ]
    return pl.pallas_call(
        functools.partial(_mixer_kernel, cfg),
        grid=(nb, nl),
        in_specs=in_specs,
        out_specs=out_specs,
        out_shape=out_shape,
        scratch_shapes=scratch,
        input_output_aliases=aliases,
        compiler_params=pltpu.CompilerParams(
            dimension_semantics=("arbitrary", "arbitrary"), vmem_limit_bytes=VMEM_LIMIT_BYTES),
        name="mixer",
    )(*operands)


FfnCfg = collections.namedtuple("FfnCfg", "T bb nl final zero_init n_alias")


def _ffn_kernel(cfg, *refs):
    T, bb, nl = cfg.T, cfg.bb, cfg.nl
    refs = list(refs)
    x_ref = refs.pop(0)
    sffn_ref = None if cfg.zero_init else refs.pop(0)
    n2_ref, wa_ref, wg_ref, fcw_ref, fcb_ref, wo_ref, fn_ref = refs[:7]
    del refs[:7]
    del refs[:cfg.n_alias]
    xo_ref, offn_ref, abuf, act = refs

    l = pl.program_id(1)
    R = bb * T
    x = x_ref[...]
    xn = (_rms_scale(x) * n2_ref[...]).astype(BF16)
    lo = FFNC_BASE - (FFN_CONV - 1)

    def init_hist():
        if cfg.zero_init:
            abuf[:, lo:FFNC_BASE, :] = jnp.zeros((bb, FFN_CONV - 1, D_FF), F32)
        else:
            abuf[:, lo:FFNC_BASE, :] = sffn_ref[...]

    if nl == 1:
        init_hist()
    else:
        @pl.when(l == 0)
        def _():
            init_hist()

        @pl.when(l > 0)
        def _():
            abuf[:, lo:FFNC_BASE, :] = offn_ref[...]
    f0 = 0
    for fw in FFN_SPLITS:
        a = _dot(xn, wa_ref[:, f0:f0 + fw])
        g = _dot(xn, wg_ref[:, f0:f0 + fw])
        abuf[:, FFNC_BASE:FFNC_BASE + T, f0:f0 + fw] = a.reshape(bb, T, fw)
        ac = fcb_ref[:, f0:f0 + fw][None]
        for kk in range(FFN_CONV):
            ac = ac + fcw_ref[kk:kk + 1, f0:f0 + fw][None] * abuf[:, lo + kk:lo + kk + T, f0:f0 + fw]
        offn_ref[:, :, f0:f0 + fw] = abuf[:, T + lo:T + FFNC_BASE, f0:f0 + fw]
        act[:, f0:f0 + fw] = (_silu(ac).reshape(R, fw) * g).astype(BF16)
        f0 += fw
    y = x + _dot(act[...], wo_ref[...])
    if cfg.final:
        y = _rms_scale(y) * fn_ref[...]
    xo_ref[...] = y


def _ffn_call(x2d, state_in, state_prev, wts, layer, Bn, L, bb, T, final):
    assert L % T == 0 and Bn % bb == 0
    nl = L // T
    assert bb == 1 or nl == 1
    nb = Bn // bb
    R = bb * T
    n2, w_ffn_in, fcw, fcb, w_ffn_out, fn = wts
    x_spec = pl.BlockSpec((R, D_MODEL), lambda i, l: (i * nl + l, 0))
    st_shape = jax.ShapeDtypeStruct((DEPTH, Bn, FFN_CONV - 1, D_FF), F32)
    operands = [x2d]
    in_specs = [x_spec]
    if state_in is not None:
        operands.append(state_in)
        in_specs.append(_state_spec(state_in, layer, bb))
    operands += [n2, w_ffn_in, w_ffn_in, fcw, fcb, w_ffn_out, fn]
    in_specs += [_layer_spec(n2, layer),
                 _layer_spec(w_ffn_in, layer, block=(D_MODEL, D_FF), col=0),
                 _layer_spec(w_ffn_in, layer, block=(D_MODEL, D_FF), col=1),
                 _layer_spec(fcw, layer), _layer_spec(fcb, layer), _layer_spec(w_ffn_out, layer), _const_spec(fn)]
    aliases = {}
    if state_prev is not None:
        aliases[len(operands)] = 1
        operands.append(state_prev)
        in_specs.append(pl.BlockSpec(memory_space=pl.ANY))
    cfg = FfnCfg(T, bb, nl, final, state_in is None, len(aliases))
    return pl.pallas_call(
        functools.partial(_ffn_kernel, cfg),
        grid=(nb, nl),
        in_specs=in_specs,
        out_specs=[x_spec, _state_spec(st_shape, layer, bb)],
        out_shape=[jax.ShapeDtypeStruct((Bn * L, D_MODEL), F32), st_shape],
        scratch_shapes=[pltpu.VMEM((bb, FFNC_BASE + T, D_FF), F32),
                        pltpu.VMEM((R, D_FF), BF16)],
        input_output_aliases=aliases,
        compiler_params=pltpu.CompilerParams(
            dimension_semantics=("arbitrary", "arbitrary"), vmem_limit_bytes=VMEM_LIMIT_BYTES),
        name="ffn",
    )(*operands)


def _prep_weights(norm1, w_in, w_gk, b_gk, gla_norm, conf_w, conf_b, conf_ln_w, conf_ln_b,
                  ssm_conv_w, ssm_conv_b, dt_bias, a_log, d_skip, ssm_norm, w_out,
                  norm2, w_ffn_in, ffn_conv_w, ffn_conv_b, w_ffn_out, final_norm):
    offs = np.concatenate([[0], np.cumsum(IN_SIZES)])
    wb = w_in.astype(BF16)
    seg = lambda i: wb[:, :, int(offs[i]):int(offs[i + 1])]
    zpad = lambda n: jnp.zeros((DEPTH, D_MODEL, n), BF16)
    w_in_p = jnp.concatenate([
        seg(0), zpad(QK_PAD - GLA_QK), seg(1), zpad(QK_PAD - GLA_QK), seg(2), seg(3),
        seg(5), seg(6), seg(7), seg(8), jnp.repeat(seg(9), SSM_HEADDIM, axis=2),
        seg(4), zpad(LANE - GLA_RANK)], axis=2)
    assert w_in_p.shape == (DEPTH, D_MODEL, NP)
    row = lambda v: v.reshape(DEPTH, 1, -1).astype(F32)
    wgk_p = jnp.zeros((DEPTH, LANE, QK_PAD), F32).at[:, :GLA_RANK, :GLA_QK].set(w_gk).astype(BF16)
    bgk_p = jnp.zeros((DEPTH, 1, QK_PAD), F32).at[:, 0, :GLA_QK].set(b_gk)
    rep64 = lambda v: row(jnp.repeat(v, SSM_HEADDIM, axis=1))
    mixer_w = (row(norm1), w_in_p, wgk_p, bgk_p, row(jnp.tile(gla_norm, (1, GLA_HEADS))),
               conf_w.astype(F32), row(conf_b), row(conf_ln_w), row(conf_ln_b),
               ssm_conv_w.astype(F32), row(ssm_conv_b), rep64(dt_bias), rep64(a_log),
               rep64(d_skip), row(ssm_norm), w_out.astype(BF16))
    ffn_w = (row(norm2), w_ffn_in.astype(BF16), ffn_conv_w.astype(F32), row(ffn_conv_b),
             w_ffn_out.astype(BF16), final_norm.reshape(1, -1).astype(F32))
    return mixer_w, ffn_w


def _run_group(x, states, mixer_w, ffn_w, bb, T, C):
    Bn, L, _ = x.shape
    x2d = x.reshape(Bn * L, D_MODEL)
    out_states = None
    for layer in range(DEPTH):
        prev4 = None if out_states is None else out_states[:4]
        prev_ffn = None if out_states is None else out_states[4]
        x2d, ngla, nconf, nsc, nssm = _mixer_call(
            x2d, None if states is None else states[:4], prev4, mixer_w, layer, Bn, L, bb, T, C)
        x2d, nffn = _ffn_call(x2d, None if states is None else states[4], prev_ffn, ffn_w, layer, Bn, L, bb, T,
                              final=(layer == DEPTH - 1))
        out_states = (ngla, nconf, nsc, nssm, nffn)
    return x2d.reshape(Bn, L, D_MODEL), out_states


def _tiles(Lp, Bs, Ls):
    tp = min(512, Lp)
    return tp, math.gcd(tp, GLA_CHUNK), min(16, Bs), math.gcd(Ls, GLA_CHUNK)


def kernel(x_prompt, x_sample, state_gla, state_conf_conv, state_ssm_conv, state_ssm, state_ffn_conv, norm1, w_in, w_gk, b_gk, gla_norm, conf_w, conf_b, conf_ln_w, conf_ln_b, ssm_conv_w, ssm_conv_b, dt_bias, a_log, d_skip, ssm_norm, w_out, norm2, w_ffn_in, ffn_conv_w, ffn_conv_b, w_ffn_out, final_norm):
    mixer_w, ffn_w = _prep_weights(norm1, w_in, w_gk, b_gk, gla_norm, conf_w, conf_b, conf_ln_w, conf_ln_b,
                                   ssm_conv_w, ssm_conv_b, dt_bias, a_log, d_skip, ssm_norm, w_out,
                                   norm2, w_ffn_in, ffn_conv_w, ffn_conv_b, w_ffn_out, final_norm)
    tp, cp, bbs, cs = _tiles(x_prompt.shape[1], x_sample.shape[0], x_sample.shape[1])
    y_p, st_p = _run_group(x_prompt, None, mixer_w, ffn_w, 1, tp, cp)
    s_states = (state_gla, state_conf_conv, state_ssm_conv, state_ssm, state_ffn_conv)
    y_s, st_s = _run_group(x_sample, s_states, mixer_w, ffn_w, bbs, x_sample.shape[1], cs)
    return (y_p, y_s) + tuple(st_p) + tuple(st_s)
```

```python
import collections
import functools
import math

import numpy as np
import jax
import jax.numpy as jnp
from jax import lax
from jax.experimental import pallas as pl
from jax.experimental.pallas import tpu as pltpu

F32 = jnp.float32
BF16 = jnp.bfloat16

D_MODEL = 1024
DEPTH = 2
GLA_HEADS = 6
GLA_DK = 32
GLA_DV = 64
GLA_QK = GLA_HEADS * GLA_DK
GLA_WIDTH = GLA_HEADS * GLA_DV
GLA_RANK = 16
GLA_GATE_NORM = 16.0
GLA_CHUNK = 64
CONF_CH = 256
CONF_K = 31
LN_EPS = 1e-5
SSM_WIDTH = 384
SSM_HEADDIM = 64
SSM_HEADS = 6
SSM_GROUPS = 2
SSM_HPG = 3
SSM_STATE = 64
SSM_GN = SSM_GROUPS * SSM_STATE
SSM_XBC = SSM_WIDTH + 2 * SSM_GN
SSM_CONV = 4
SSM_CHUNK = 64
D_FF = 2816
FFN_CONV = 3
EPS = 1e-6
IN_SIZES = (GLA_QK, GLA_QK, GLA_WIDTH, GLA_WIDTH, GLA_RANK, CONF_CH, CONF_CH, SSM_WIDTH, SSM_XBC, SSM_HEADS)

LANE = 128
SUBLANE = 8
MXU_DIM = 256
VMEM_LIMIT_BYTES = 56 * 1024 * 1024

QK_PAD = 256
Q0 = 0
K0 = Q0 + QK_PAD
V0 = K0 + QK_PAD
GO0 = V0 + GLA_WIDTH
CA0 = GO0 + GLA_WIDTH
CB0 = CA0 + CONF_CH
Z0 = CB0 + CONF_CH
XBC0 = Z0 + SSM_WIDTH
DT0 = XBC0 + SSM_XBC
GLR0 = DT0 + SSM_WIDTH
NP = GLR0 + LANE

SPLIT_PARTS = 2

FFN_SPLITS = (768, 768, 768, 512)

CONF_BASE = 32
SSMC_BASE = 8
FFNC_BASE = 8

CHUNK_STAGE_LAG = 3
SEQ_STAGES = 6
SEQ_STAGE_LAG = 1
SEQ_SLOTS = -(-SEQ_STAGES // SEQ_STAGE_LAG)
PROJ_PIECES_PER_TICK = 2


def _dot(a, b):
    return jnp.dot(a, b, preferred_element_type=F32)


def _dot_nt(a, b):
    return lax.dot_general(a, b, (((1,), (1,)), ((), ())), preferred_element_type=F32)


def _dot_tn(a, b):
    return lax.dot_general(a, b, (((0,), (0,)), ((), ())), preferred_element_type=F32)


def _split_bf16(a, parts):
    out = []
    r = a
    for i in range(parts):
        p = r.astype(BF16)
        out.append(p)
        if i + 1 < parts:
            r = r - p.astype(F32)
    return out


def _sigmoid(x):
    return 1.0 / (1.0 + jnp.exp(-x))


def _silu(x):
    return x * _sigmoid(x)


def _softplus(x):
    return jnp.maximum(x, 0.0) + jnp.log(1.0 + jnp.exp(-jnp.abs(x)))


def _log_sigmoid(x):
    return jnp.minimum(x, 0.0) - jnp.log(1.0 + jnp.exp(-jnp.abs(x)))


def _rms_scale(x):
    return x * lax.rsqrt(jnp.mean(x * x, axis=-1, keepdims=True) + EPS)


def _per_head_matmul(cat, rhs, c):
    first_head = lax.broadcasted_iota(jnp.int32, (c, LANE), 1) < 64
    packed = c % (2 * SUBLANE) == 0
    if packed:
        keep_lo = jnp.where(first_head, 1.0, 0.0).astype(BF16)
        keep_hi = jnp.where(first_head, 0.0, 1.0).astype(BF16)
    cols = []
    for col in range(3):
        r = rhs[:, col * LANE:(col + 1) * LANE]
        if packed:
            rb = r.astype(BF16)
            two = jnp.concatenate([rb * keep_lo, rb * keep_hi], axis=0)
        else:
            two = jnp.concatenate([jnp.where(first_head, r, 0.0), jnp.where(first_head, 0.0, r)],
                                  axis=0).astype(BF16)
        cols.append(_dot(cat[:, 2 * col * c:(2 * col + 2) * c], two))
    return jnp.concatenate(cols, axis=1)


def _run_staged(chunk_gens, lag, fillers=None, before_start=None, after_finish=None):
    pending = list(enumerate(chunk_gens))
    active = []
    tick = 0
    while pending or active:
        if pending and tick % lag == 0:
            i, g = pending.pop(0)
            if before_start is not None:
                before_start(i)
            active.append((i, g))
        for item in list(active):
            i, g = item
            try:
                next(g)
            except StopIteration:
                active.remove(item)
                if after_finish is not None:
                    after_finish(i)
        if fillers is not None:
            next(fillers, None)
        tick += 1


MixerCfg = collections.namedtuple("MixerCfg", "C T bb nl zero_init n_alias prefetch")


def _mixer_kernel(cfg, *refs):
    C, T, bb, nl = cfg.C, cfg.T, cfg.bb, cfg.nl
    refs = list(refs)
    x_ref = refs.pop(0)
    xnext_ref = refs.pop(0) if cfg.prefetch else None
    if cfg.zero_init:
        sgla_ref = sconf_ref = ssc_ref = sssm_ref = None
    else:
        sgla_ref, sconf_ref, ssc_ref, sssm_ref = refs[:4]
        del refs[:4]
    (n1_ref, win_ref, wgk_ref, bgk_ref, gnorm_ref, cw_ref, cbias_ref, lnw_ref, lnb_ref,
     scw_ref, scb_ref, dtb_ref, alog_ref, dskip_ref, snorm_ref, wout_ref) = refs[:16]
    del refs[:16]
    (tril_ref, trilcat_ref, hm6_ref, eyecat_ref, gm6_ref, stmask_ref, htmask_ref) = refs[:7]
    del refs[:7]
    del refs[:cfg.n_alias]
    xo_ref, ogla_ref, oconf_ref, osc_ref, ossm_ref = refs[:5]
    proj_scr, xn_scr, st_scr, ht_scr, ubuf, xbuf, mix_scr = refs[5:]

    l = pl.program_id(1)
    nchunk = T // C
    ngroups = 2 if (bb == 1 and nchunk % 2 == 0) else 1
    RG = bb * T // ngroups
    NPIECE = NP // MXU_DIM

    def in_proj_full(g):
        rows = slice(g * RG, (g + 1) * RG)
        xn = _rms_scale(x_ref[rows, :]) * n1_ref[...]
        proj_scr[rows, :] = _dot(xn.astype(BF16), win_ref[...])

    def in_proj_pieces(g, src_ref=None):
        rows = slice(g * RG, (g + 1) * RG)
        src = x_ref[rows, :] if src_ref is None else src_ref[...]
        xn_scr[...] = (_rms_scale(src) * n1_ref[...]).astype(BF16)
        for p in range(NPIECE):
            cols = slice(p * MXU_DIM, (p + 1) * MXU_DIM)
            proj_scr[rows, cols] = _dot(xn_scr[...], win_ref[:, cols])
            if (p + 1) % PROJ_PIECES_PER_TICK == 0:
                yield

    def out_proj_pieces(g):
        rows = slice(g * RG, (g + 1) * RG)
        for p in range(D_MODEL // MXU_DIM):
            cols = slice(p * MXU_DIM, (p + 1) * MXU_DIM)
            xo_ref[rows, cols] = x_ref[rows, cols] + _dot(mix_scr[rows, :].astype(BF16), wout_ref[:, cols])
            yield

    tril = tril_ref[...]
    trilcat = trilcat_ref[...] > 0.5

    def cumsum_rows(a):
        acc = None
        for p in _split_bf16(a, SPLIT_PARTS):
            t = _dot(tril, p)
            acc = t if acc is None else acc + t
        return acc

    first_half = lax.broadcasted_iota(jnp.int32, (C, LANE), 1) < 64

    def stack_masked(a, mask_ref, n):
        if C % (2 * SUBLANE) == 0:
            return jnp.concatenate([a.astype(BF16)] * n, axis=0) * mask_ref[...]
        return (jnp.concatenate([a] * n, axis=0) * mask_ref[...].astype(F32)).astype(BF16)

    def head_mean(a):
        cols = []
        for col in range(3):
            blk = a[:, col * LANE:(col + 1) * LANE]
            lo = jnp.sum(jnp.where(first_half, blk, 0.0), axis=-1, keepdims=True)
            hi = jnp.sum(jnp.where(first_half, 0.0, blk), axis=-1, keepdims=True)
            cols.append(jnp.where(first_half, lo, hi))
        return jnp.concatenate(cols, axis=1) * (1.0 / 64)

    def group_mean(a):
        c0, c1, c2 = (a[:, col * LANE:(col + 1) * LANE] for col in range(3))
        g0 = jnp.sum(c0 + jnp.where(first_half, c1, 0.0), axis=-1, keepdims=True)
        g1 = jnp.sum(c2 + jnp.where(first_half, 0.0, c1), axis=-1, keepdims=True)
        shape = (C, LANE)
        return jnp.concatenate([jnp.broadcast_to(g0, shape), jnp.where(first_half, g0, g1),
                                jnp.broadcast_to(g1, shape)], axis=1) * (1.0 / (SSM_HPG * SSM_HEADDIM))

    def chunk_stages(r0, c0, slot):
        def P(off, width):
            return proj_scr[pl.ds(r0, C), off:off + width]

        gk_pre = _dot(P(GLR0, LANE).astype(BF16), wgk_ref[...]) + bgk_ref[...]
        xbc = P(XBC0, SSM_XBC)
        ext = jnp.concatenate([xbuf[slot, pl.ds(c0, SUBLANE), :], xbc], axis=0)
        xbuf[slot, pl.ds(SSMC_BASE + c0, C), :] = xbc
        acc = jnp.broadcast_to(scb_ref[...], (C, SSM_XBC))
        for kk in range(SSM_CONV):
            lo = SUBLANE - (SSM_CONV - 1) + kk
            acc = acc + scw_ref[kk:kk + 1, :] * ext[lo:lo + C, :]
        xcv = _silu(acc)
        xs = xcv[:, 0:SSM_WIDTH]
        bm = xcv[:, SSM_WIDTH:SSM_WIDTH + SSM_GN]
        cm = xcv[:, SSM_WIDTH + SSM_GN:SSM_XBC]
        dtf = _softplus(P(DT0, SSM_WIDTH) + dtb_ref[...])
        af = dtf * (-jnp.exp(alog_ref[...]))
        yield

        gk = _log_sigmoid(gk_pre) * (1.0 / GLA_GATE_NORM)
        acum = cumsum_rows(af)
        b = cumsum_rows(gk)
        u = P(CA0, CONF_CH) * _sigmoid(P(CB0, CONF_CH))
        b0 = CONF_BASE + c0
        uext = jnp.concatenate([ubuf[slot, 0, pl.ds(b0 - SUBLANE, SUBLANE), :], u], axis=0)
        ubuf[slot, 0, pl.ds(b0, C), :] = u
        for s in range(1, SUBLANE):
            ubuf[slot, s, pl.ds(b0 - SUBLANE, C), :] = uext[s:s + C, :]
        xdt = xs * dtf
        bbd = stack_masked(bm, gm6_ref, SSM_HEADS)
        cb_cat = _dot_nt(cm.astype(BF16), bbd)

        def conf_taps(acc, k_lo, k_hi):
            for kk in range(k_lo, k_hi):
                s = (kk - (CONF_K - 1)) % SUBLANE
                off = kk - (CONF_K - 1) - s
                acc = acc + cw_ref[kk:kk + 1, :] * ubuf[slot, s, pl.ds(b0 + off, C), :]
            return acc
        yield

        alast = acum[C - 1:C, :]
        if C == SSM_CHUNK:
            col_cat = acum
            acum_t = acum.T
            row_cat = jnp.concatenate([acum_t[h * 64:h * 64 + 1, :] for h in range(SSM_HEADS)], axis=1)
        else:
            col_cat = jnp.concatenate([acum[:, h * 64:h * 64 + C] for h in range(SSM_HEADS)], axis=1)
            ones_cc = jnp.ones((C, C), BF16)
            row_cat = None
            for p in _split_bf16(col_cat * eyecat_ref[...], SPLIT_PARTS):
                t = _dot(ones_cc, p)
                row_cat = t if row_cat is None else row_cat + t
        lmat = jnp.exp(jnp.where(trilcat, col_cat - row_cat, -jnp.inf))
        blast = b[C - 1:C, :]
        q_in = P(Q0, QK_PAD) * (GLA_DK ** -0.5) * jnp.exp(b)
        k = P(K0, QK_PAD)
        k_in = k * jnp.exp(-b)
        k_out = k * jnp.exp(blast - b)
        v = P(V0, GLA_WIDTH)
        vb = v.astype(BF16)
        q_inb = q_in.astype(BF16)
        kbd = stack_masked(k_in, hm6_ref, GLA_HEADS)
        att = _dot_nt(q_inb, kbd)
        cacc = conf_taps(jnp.broadcast_to(cbias_ref[...], (C, CONF_CH)), 0, 10)
        yield

        att = jnp.where(trilcat, att, 0.0)
        o_intra = _per_head_matmul(att.astype(BF16), v, C)
        y_intra = _per_head_matmul((cb_cat * lmat).astype(BF16), xdt, C)
        k_outb = k_out.astype(BF16)
        decay = jnp.exp(blast)
        o_inter = []
        for r_lo, r_hi, l_lo in ((0, 4 * GLA_DV, 0), (4 * GLA_DV, GLA_WIDTH, LANE)):
            st = st_scr[slot, r_lo:r_hi, l_lo:l_lo + LANE]
            o_inter.append(_dot_nt(q_inb[:, l_lo:l_lo + LANE], st.astype(BF16)))
            kv = _dot_tn(vb[:, r_lo:r_hi], k_outb[:, l_lo:l_lo + LANE])
            st_scr[slot, r_lo:r_hi, l_lo:l_lo + LANE] = (
                st * decay[:, l_lo:l_lo + LANE] + kv * stmask_ref[r_lo:r_hi, l_lo:l_lo + LANE])
        o = o_intra + jnp.concatenate(o_inter, axis=1)
        ht = ht_scr[slot]
        y_inter = _dot(cm.astype(BF16), ht.astype(BF16)) * jnp.exp(acum)
        cacc = conf_taps(cacc, 10, 20)
        wx = xdt * jnp.exp(alast - acum)
        stn = _dot_tn(bm.astype(BF16), wx.astype(BF16))
        ht_scr[slot] = ht * jnp.exp(alast) + stn * htmask_ref[...]
        yield

        ms_o = head_mean(o * o)
        y = y_intra + y_inter + dskip_ref[...] * xs
        y = y * _silu(P(Z0, SSM_WIDTH))
        ms_y = group_mean(y * y)
        cacc = conf_taps(cacc, 20, CONF_K)
        mu = jnp.mean(cacc, axis=-1, keepdims=True)
        xc = cacc - mu
        var = jnp.mean(xc * xc, axis=-1, keepdims=True)
        cval = _silu(xc * lax.rsqrt(var + LN_EPS) * lnw_ref[...] + lnb_ref[...])
        mix_scr[pl.ds(r0, C), GLA_WIDTH:GLA_WIDTH + CONF_CH] = cval.astype(mix_scr.dtype)
        o = o * lax.rsqrt(ms_o + EPS) * gnorm_ref[...] * _silu(P(GO0, GLA_WIDTH))
        mix_scr[pl.ds(r0, C), 0:GLA_WIDTH] = o.astype(mix_scr.dtype)
        y = y * lax.rsqrt(ms_y + EPS) * snorm_ref[...]
        mix_scr[pl.ds(r0, C), GLA_WIDTH + CONF_CH:D_MODEL] = y.astype(mix_scr.dtype)

    def load_conv_hist(slot, hconf, hsc):
        hist = jnp.concatenate([jnp.zeros((CONF_BASE - (CONF_K - 1), CONF_CH), F32), hconf], axis=0)
        ubuf[slot, 0, 0:CONF_BASE, :] = hist
        for s in range(1, SUBLANE):
            ubuf[slot, s, 0:CONF_BASE - SUBLANE, :] = hist[s:s + CONF_BASE - SUBLANE, :]
        xbuf[slot, 0:SSMC_BASE, :] = jnp.concatenate(
            [jnp.zeros((SSMC_BASE - (SSM_CONV - 1), SSM_XBC), F32), hsc], axis=0)

    def load_state(j, slot, first):
        if not first:
            load_conv_hist(slot, oconf_ref[j], osc_ref[j])
        elif cfg.zero_init:
            st_scr[slot] = jnp.zeros((GLA_WIDTH, QK_PAD), F32)
            ht_scr[slot] = jnp.zeros((SSM_GN, SSM_WIDTH), F32)
            load_conv_hist(slot, jnp.zeros((CONF_K - 1, CONF_CH), F32), jnp.zeros((SSM_CONV - 1, SSM_XBC), F32))
        else:
            rows = []
            for h in range(GLA_HEADS):
                parts = []
                if h > 0:
                    parts.append(jnp.zeros((GLA_DK, h * GLA_DV), F32))
                parts.append(sgla_ref[j, h])
                if h < GLA_HEADS - 1:
                    parts.append(jnp.zeros((GLA_DK, (GLA_HEADS - 1 - h) * GLA_DV), F32))
                rows.append(jnp.concatenate(parts, axis=1))
            rows.append(jnp.zeros((QK_PAD - GLA_QK, GLA_WIDTH), F32))
            st_scr[slot] = jnp.concatenate(rows, axis=0).T
            rows = []
            for h in range(SSM_HEADS):
                z = jnp.zeros((SSM_HEADDIM, SSM_STATE), F32)
                rows.append(jnp.concatenate([sssm_ref[j, h], z] if h // SSM_HPG == 0 else [z, sssm_ref[j, h]], axis=1))
            ht_scr[slot] = jnp.concatenate(rows, axis=0).T
            load_conv_hist(slot, sconf_ref[j], ssc_ref[j])

    def store_conv_state(j, slot):
        oconf_ref[j] = ubuf[slot, 0, T + CONF_BASE - (CONF_K - 1):T + CONF_BASE, :]
        osc_ref[j] = xbuf[slot, T + SSMC_BASE - (SSM_CONV - 1):T + SSMC_BASE, :]

    def store_state(j, slot):
        stt = st_scr[slot].T
        for h in range(GLA_HEADS):
            ogla_ref[j, h] = stt[h * GLA_DK:(h + 1) * GLA_DK, h * GLA_DV:(h + 1) * GLA_DV]
        htt = ht_scr[slot].T
        for h in range(SSM_HEADS):
            g = h // SSM_HPG
            ossm_ref[j, h] = htt[h * SSM_HEADDIM:(h + 1) * SSM_HEADDIM, g * SSM_STATE:(g + 1) * SSM_STATE]

    if bb == 1:
        if cfg.prefetch:
            @pl.when((pl.program_id(0) == 0) & (l == 0))
            def _():
                in_proj_full(0)
        else:
            in_proj_full(0)
        if nl == 1:
            load_state(0, 0, True)
        else:
            @pl.when(l == 0)
            def _():
                load_state(0, 0, True)

            @pl.when(l > 0)
            def _():
                load_state(0, 0, False)

        cpg = nchunk // ngroups
        work = collections.deque()
        if ngroups == 2:
            work.append(in_proj_pieces(1))

        def drain_work():
            while work:
                for _ in work.popleft():
                    pass

        def before_start(i):
            if ngroups == 2 and i == cpg:
                drain_work()

        def after_finish(i):
            if (i + 1) % cpg == 0:
                work.append(out_proj_pieces(i // cpg))
            if cfg.prefetch and i == cpg - 1:
                work.append(in_proj_pieces(0, xnext_ref))

        def work_pieces():
            while True:
                if work:
                    try:
                        next(work[0])
                    except StopIteration:
                        work.popleft()
                yield

        _run_staged([chunk_stages(ci * C, ci * C, 0) for ci in range(nchunk)], CHUNK_STAGE_LAG,
                    fillers=work_pieces(), before_start=before_start, after_finish=after_finish)
        drain_work()

        store_conv_state(0, 0)
        if nl == 1:
            store_state(0, 0)
        else:
            @pl.when(l == nl - 1)
            def _():
                store_state(0, 0)
    else:
        assert nl == 1 and nchunk == 1
        in_proj_full(0)

        def seq_stages(j):
            slot = j % SEQ_SLOTS
            load_state(j, slot, True)
            yield
            yield from chunk_stages(j * T, 0, slot)
            store_conv_state(j, slot)
            store_state(j, slot)

        _run_staged([seq_stages(j) for j in range(bb)], SEQ_STAGE_LAG)
        xo_ref[...] = x_ref[...] + _dot(mix_scr[...].astype(BF16), wout_ref[...])


def _np_consts(C):
    i = np.arange(C)
    tril = (i[None, :] <= i[:, None]).astype(np.float32)
    trilcat = np.tile(tril, (1, 6))
    eyecat = np.tile(np.eye(C, dtype=np.float32), (1, 6))
    hm6 = np.zeros((6 * C, QK_PAD), np.float32)
    gm6 = np.zeros((6 * C, SSM_GN), np.float32)
    for h in range(6):
        hm6[h * C:(h + 1) * C, h * GLA_DK:(h + 1) * GLA_DK] = 1.0
        g = h // SSM_HPG
        gm6[h * C:(h + 1) * C, g * SSM_STATE:(g + 1) * SSM_STATE] = 1.0
    stmask = np.zeros((GLA_WIDTH, QK_PAD), np.float32)
    htmask = np.zeros((SSM_GN, SSM_WIDTH), np.float32)
    for h in range(6):
        stmask[h * 64:(h + 1) * 64, h * 32:(h + 1) * 32] = 1.0
        g = h // SSM_HPG
        htmask[g * 64:(g + 1) * 64, h * 64:(h + 1) * 64] = 1.0
    return (jnp.asarray(tril, BF16), jnp.asarray(trilcat, F32), jnp.asarray(hm6, BF16), jnp.asarray(eyecat, F32),
            jnp.asarray(gm6, BF16), jnp.asarray(stmask, F32), jnp.asarray(htmask, F32))


def _const_spec(a):
    nd = a.ndim
    return pl.BlockSpec(a.shape, lambda i, l: (0,) * nd, pipeline_mode=pl.Buffered(1))


def _layer_spec(a, layer, block=None, col=0):
    shape = a.shape[1:] if block is None else block
    return pl.BlockSpec((None,) + tuple(shape), lambda i, l: (layer, 0, col), pipeline_mode=pl.Buffered(1))


def _state_spec(a, layer, bb):
    nrest = a.ndim - 2
    return pl.BlockSpec((None, bb) + tuple(a.shape[2:]), lambda i, l: (layer, i) + (0,) * nrest)


def _mixer_call(x2d, states_in, states_prev, wts, layer, Bn, L, bb, T, C):
    assert L % T == 0 and T % C == 0 and Bn % bb == 0 and C % SUBLANE == 0
    nl = L // T
    assert bb == 1 or nl == 1
    nb = Bn // bb
    R = bb * T
    nslot = 1 if bb == 1 else SEQ_SLOTS
    consts = _np_consts(C)
    x_spec = pl.BlockSpec((R, D_MODEL), lambda i, l: (i * nl + l, 0))
    state_shapes = ((DEPTH, Bn, GLA_HEADS, GLA_DK, GLA_DV), (DEPTH, Bn, CONF_K - 1, CONF_CH),
                    (DEPTH, Bn, SSM_CONV - 1, SSM_XBC), (DEPTH, Bn, SSM_HEADS, SSM_HEADDIM, SSM_STATE))
    out_shape = ([jax.ShapeDtypeStruct((Bn * L, D_MODEL), F32)]
                 + [jax.ShapeDtypeStruct(s, F32) for s in state_shapes])
    out_specs = [x_spec] + [_state_spec(s, layer, bb) for s in out_shape[1:]]

    ngroups = 2 if (bb == 1 and (T // C) % 2 == 0) else 1
    prefetch = ngroups == 2
    operands = [x2d]
    in_specs = [x_spec]
    if prefetch:
        rg = R // ngroups
        last_block = (Bn * L) // rg - ngroups
        operands.append(x2d)
        in_specs.append(pl.BlockSpec(
            (rg, D_MODEL), lambda i, l: (jnp.minimum((i * nl + l + 1) * ngroups, last_block), 0)))
    if states_in is not None:
        operands += list(states_in)
        in_specs += [_state_spec(a, layer, bb) for a in states_in]
    operands += list(wts)
    in_specs += [_layer_spec(a, layer) for a in wts]
    operands += list(consts)
    in_specs += [_const_spec(a) for a in consts]
    aliases = {}
    if states_prev is not None:
        for k, a in enumerate(states_prev):
            aliases[len(operands)] = 1 + k
            operands.append(a)
            in_specs.append(pl.BlockSpec(memory_space=pl.ANY))
    cfg = MixerCfg(C, T, bb, nl, states_in is None, len(aliases), prefetch)
    scratch = [
        pltpu.VMEM((R, NP), F32),
        pltpu.VMEM((R // ngroups, D_MODEL), BF16),
        pltpu.VMEM((nslot, GLA_WIDTH, QK_PAD), F32),
        pltpu.VMEM((nslot, SSM_GN, SSM_WIDTH), F32),
        pltpu.VMEM((nslot, SUBLANE, CONF_BASE + T, CONF_CH), F32),
        pltpu.VMEM((nslot, SSMC_BASE + T, SSM_XBC), F32),
        pltpu.VMEM((R, D_MODEL), BF16 if C % (2 * SUBLANE) == 0 else F32),
    ]
    return pl.pallas_call(
        functools.partial(_mixer_kernel, cfg),
        grid=(nb, nl),
        in_specs=in_specs,
        out_specs=out_specs,
        out_shape=out_shape,
        scratch_shapes=scratch,
        input_output_aliases=aliases,
        compiler_params=pltpu.CompilerParams(
            dimension_semantics=("arbitrary", "arbitrary"), vmem_limit_bytes=VMEM_LIMIT_BYTES),
        name="mixer",
    )(*operands)


FfnCfg = collections.namedtuple("FfnCfg", "T bb nl final zero_init n_alias")


def _ffn_kernel(cfg, *refs):
    T, bb, nl = cfg.T, cfg.bb, cfg.nl
    refs = list(refs)
    x_ref = refs.pop(0)
    sffn_ref = None if cfg.zero_init else refs.pop(0)
    n2_ref, wa_ref, wg_ref, fcw_ref, fcb_ref, wo_ref, fn_ref = refs[:7]
    del refs[:7]
    del refs[:cfg.n_alias]
    xo_ref, offn_ref, abuf, act = refs

    l = pl.program_id(1)
    R = bb * T
    x = x_ref[...]
    xn = (_rms_scale(x) * n2_ref[...]).astype(BF16)
    lo = FFNC_BASE - (FFN_CONV - 1)

    def init_hist():
        if cfg.zero_init:
            abuf[:, lo:FFNC_BASE, :] = jnp.zeros((bb, FFN_CONV - 1, D_FF), F32)
        else:
            abuf[:, lo:FFNC_BASE, :] = sffn_ref[...]

    if nl == 1:
        init_hist()
    else:
        @pl.when(l == 0)
        def _():
            init_hist()

        @pl.when(l > 0)
        def _():
            abuf[:, lo:FFNC_BASE, :] = offn_ref[...]
    f0 = 0
    for fw in FFN_SPLITS:
        a = _dot(xn, wa_ref[:, f0:f0 + fw])
        g = _dot(xn, wg_ref[:, f0:f0 + fw])
        abuf[:, FFNC_BASE:FFNC_BASE + T, f0:f0 + fw] = a.reshape(bb, T, fw)
        ac = fcb_ref[:, f0:f0 + fw][None]
        for kk in range(FFN_CONV):
            ac = ac + fcw_ref[kk:kk + 1, f0:f0 + fw][None] * abuf[:, lo + kk:lo + kk + T, f0:f0 + fw]
        offn_ref[:, :, f0:f0 + fw] = abuf[:, T + lo:T + FFNC_BASE, f0:f0 + fw]
        act[:, f0:f0 + fw] = (_silu(ac).reshape(R, fw) * g).astype(BF16)
        f0 += fw
    y = x + _dot(act[...], wo_ref[...])
    if cfg.final:
        y = _rms_scale(y) * fn_ref[...]
    xo_ref[...] = y


def _ffn_call(x2d, state_in, state_prev, wts, layer, Bn, L, bb, T, final):
    assert L % T == 0 and Bn % bb == 0
    nl = L // T
    assert bb == 1 or nl == 1
    nb = Bn // bb
    R = bb * T
    n2, w_ffn_in, fcw, fcb, w_ffn_out, fn = wts
    x_spec = pl.BlockSpec((R, D_MODEL), lambda i, l: (i * nl + l, 0))
    st_shape = jax.ShapeDtypeStruct((DEPTH, Bn, FFN_CONV - 1, D_FF), F32)
    operands = [x2d]
    in_specs = [x_spec]
    if state_in is not None:
        operands.append(state_in)
        in_specs.append(_state_spec(state_in, layer, bb))
    operands += [n2, w_ffn_in, w_ffn_in, fcw, fcb, w_ffn_out, fn]
    in_specs += [_layer_spec(n2, layer),
                 _layer_spec(w_ffn_in, layer, block=(D_MODEL, D_FF), col=0),
                 _layer_spec(w_ffn_in, layer, block=(D_MODEL, D_FF), col=1),
                 _layer_spec(fcw, layer), _layer_spec(fcb, layer), _layer_spec(w_ffn_out, layer), _const_spec(fn)]
    aliases = {}
    if state_prev is not None:
        aliases[len(operands)] = 1
        operands.append(state_prev)
        in_specs.append(pl.BlockSpec(memory_space=pl.ANY))
    cfg = FfnCfg(T, bb, nl, final, state_in is None, len(aliases))
    return pl.pallas_call(
        functools.partial(_ffn_kernel, cfg),
        grid=(nb, nl),
        in_specs=in_specs,
        out_specs=[x_spec, _state_spec(st_shape, layer, bb)],
        out_shape=[jax.ShapeDtypeStruct((Bn * L, D_MODEL), F32), st_shape],
        scratch_shapes=[pltpu.VMEM((bb, FFNC_BASE + T, D_FF), F32),
                        pltpu.VMEM((R, D_FF), BF16)],
        input_output_aliases=aliases,
        compiler_params=pltpu.CompilerParams(
            dimension_semantics=("arbitrary", "arbitrary"), vmem_limit_bytes=VMEM_LIMIT_BYTES),
        name="ffn",
    )(*operands)


def _prep_weights(norm1, w_in, w_gk, b_gk, gla_norm, conf_w, conf_b, conf_ln_w, conf_ln_b,
                  ssm_conv_w, ssm_conv_b, dt_bias, a_log, d_skip, ssm_norm, w_out,
                  norm2, w_ffn_in, ffn_conv_w, ffn_conv_b, w_ffn_out, final_norm):
    offs = np.concatenate([[0], np.cumsum(IN_SIZES)])
    wb = w_in.astype(BF16)
    seg = lambda i: wb[:, :, int(offs[i]):int(offs[i + 1])]
    zpad = lambda n: jnp.zeros((DEPTH, D_MODEL, n), BF16)
    w_in_p = jnp.concatenate([
        seg(0), zpad(QK_PAD - GLA_QK), seg(1), zpad(QK_PAD - GLA_QK), seg(2), seg(3),
        seg(5), seg(6), seg(7), seg(8), jnp.repeat(seg(9), SSM_HEADDIM, axis=2),
        seg(4), zpad(LANE - GLA_RANK)], axis=2)
    assert w_in_p.shape == (DEPTH, D_MODEL, NP)
    row = lambda v: v.reshape(DEPTH, 1, -1).astype(F32)
    wgk_p = jnp.zeros((DEPTH, LANE, QK_PAD), F32).at[:, :GLA_RANK, :GLA_QK].set(w_gk).astype(BF16)
    bgk_p = jnp.zeros((DEPTH, 1, QK_PAD), F32).at[:, 0, :GLA_QK].set(b_gk)
    rep64 = lambda v: row(jnp.repeat(v, SSM_HEADDIM, axis=1))
    mixer_w = (row(norm1), w_in_p, wgk_p, bgk_p, row(jnp.tile(gla_norm, (1, GLA_HEADS))),
               conf_w.astype(F32), row(conf_b), row(conf_ln_w), row(conf_ln_b),
               ssm_conv_w.astype(F32), row(ssm_conv_b), rep64(dt_bias), rep64(a_log),
               rep64(d_skip), row(ssm_norm), w_out.astype(BF16))
    ffn_w = (row(norm2), w_ffn_in.astype(BF16), ffn_conv_w.astype(F32), row(ffn_conv_b),
             w_ffn_out.astype(BF16), final_norm.reshape(1, -1).astype(F32))
    return mixer_w, ffn_w


def _run_group(x, states, mixer_w, ffn_w, bb, T, C, ffn_bb=None, ffn_T=None):
    ffn_bb = bb if ffn_bb is None else ffn_bb
    ffn_T = T if ffn_T is None else ffn_T
    Bn, L, _ = x.shape
    x2d = x.reshape(Bn * L, D_MODEL)
    out_states = None
    for layer in range(DEPTH):
        prev4 = None if out_states is None else out_states[:4]
        prev_ffn = None if out_states is None else out_states[4]
        x2d, ngla, nconf, nsc, nssm = _mixer_call(
            x2d, None if states is None else states[:4], prev4, mixer_w, layer, Bn, L, bb, T, C)
        x2d, nffn = _ffn_call(x2d, None if states is None else states[4], prev_ffn, ffn_w, layer, Bn, L,
                              ffn_bb, ffn_T, final=(layer == DEPTH - 1))
        out_states = (ngla, nconf, nsc, nssm, nffn)
    return x2d.reshape(Bn, L, D_MODEL), out_states


FFN_ROWS = 512


def _tiles(Lp, Bs, Ls):
    tp = min(512, Lp)
    return tp, math.gcd(tp, GLA_CHUNK), min(16, Bs), math.gcd(Ls, GLA_CHUNK), min(max(FFN_ROWS // Ls, 1), Bs)


def kernel(x_prompt, x_sample, state_gla, state_conf_conv, state_ssm_conv, state_ssm, state_ffn_conv, norm1, w_in, w_gk, b_gk, gla_norm, conf_w, conf_b, conf_ln_w, conf_ln_b, ssm_conv_w, ssm_conv_b, dt_bias, a_log, d_skip, ssm_norm, w_out, norm2, w_ffn_in, ffn_conv_w, ffn_conv_b, w_ffn_out, final_norm):
    mixer_w, ffn_w = _prep_weights(norm1, w_in, w_gk, b_gk, gla_norm, conf_w, conf_b, conf_ln_w, conf_ln_b,
                                   ssm_conv_w, ssm_conv_b, dt_bias, a_log, d_skip, ssm_norm, w_out,
                                   norm2, w_ffn_in, ffn_conv_w, ffn_conv_b, w_ffn_out, final_norm)
    tp, cp, bbs, cs, bbf = _tiles(x_prompt.shape[1], x_sample.shape[0], x_sample.shape[1])
    y_p, st_p = _run_group(x_prompt, None, mixer_w, ffn_w, 1, tp, cp)
    s_states = (state_gla, state_conf_conv, state_ssm_conv, state_ssm, state_ffn_conv)
    y_s, st_s = _run_group(x_sample, s_states, mixer_w, ffn_w, bbs, x_sample.shape[1], cs, ffn_bb=bbf)
    return (y_p, y_s) + tuple(st_p) + tuple(st_s)
```

```python
import collections
import functools
import math

import numpy as np
import jax
import jax.numpy as jnp
from jax import lax
from jax.experimental import pallas as pl
from jax.experimental.pallas import tpu as pltpu

F32 = jnp.float32
BF16 = jnp.bfloat16

D_MODEL = 1024
DEPTH = 2
GLA_HEADS = 6
GLA_DK = 32
GLA_DV = 64
GLA_QK = GLA_HEADS * GLA_DK
GLA_WIDTH = GLA_HEADS * GLA_DV
GLA_RANK = 16
GLA_GATE_NORM = 16.0
GLA_CHUNK = 64
CONF_CH = 256
CONF_K = 31
LN_EPS = 1e-5
SSM_WIDTH = 384
SSM_HEADDIM = 64
SSM_HEADS = 6
SSM_GROUPS = 2
SSM_HPG = 3
SSM_STATE = 64
SSM_GN = SSM_GROUPS * SSM_STATE
SSM_XBC = SSM_WIDTH + 2 * SSM_GN
SSM_CONV = 4
SSM_CHUNK = 64
D_FF = 2816
FFN_CONV = 3
EPS = 1e-6
IN_SIZES = (GLA_QK, GLA_QK, GLA_WIDTH, GLA_WIDTH, GLA_RANK, CONF_CH, CONF_CH, SSM_WIDTH, SSM_XBC, SSM_HEADS)

LANE = 128
SUBLANE = 8
MXU_DIM = 256
VMEM_LIMIT_BYTES = 56 * 1024 * 1024

QK_PAD = 256
Q0 = 0
K0 = Q0 + QK_PAD
V0 = K0 + QK_PAD
GO0 = V0 + GLA_WIDTH
CA0 = GO0 + GLA_WIDTH
CB0 = CA0 + CONF_CH
Z0 = CB0 + CONF_CH
XBC0 = Z0 + SSM_WIDTH
DT0 = XBC0 + SSM_XBC
GLR0 = DT0 + SSM_WIDTH
NP = GLR0 + LANE

SPLIT_PARTS = 2

FFN_SPLITS = (768, 768, 768, 512)

CONF_BASE = 32
SSMC_BASE = 8
FFNC_BASE = 8

CHUNK_STAGE_LAG = 3
SEQ_STAGES = 6
SEQ_STAGE_LAG = 1
SEQ_SLOTS = -(-SEQ_STAGES // SEQ_STAGE_LAG)
OUT_PROJ_CHUNKS = 2
PROJ_PIECES_PER_TICK = 2


def _dot(a, b):
    return jnp.dot(a, b, preferred_element_type=F32)


def _dot_nt(a, b):
    return lax.dot_general(a, b, (((1,), (1,)), ((), ())), preferred_element_type=F32)


def _dot_tn(a, b):
    return lax.dot_general(a, b, (((0,), (0,)), ((), ())), preferred_element_type=F32)


def _split_bf16(a, parts):
    out = []
    r = a
    for i in range(parts):
        p = r.astype(BF16)
        out.append(p)
        if i + 1 < parts:
            r = r - p.astype(F32)
    return out


def _sigmoid(x):
    return 1.0 / (1.0 + jnp.exp(-x))


def _silu(x):
    return x * _sigmoid(x)


def _softplus(x):
    return jnp.maximum(x, 0.0) + jnp.log(1.0 + jnp.exp(-jnp.abs(x)))


def _log_sigmoid(x):
    return jnp.minimum(x, 0.0) - jnp.log(1.0 + jnp.exp(-jnp.abs(x)))


def _rms_scale(x):
    return x * lax.rsqrt(jnp.mean(x * x, axis=-1, keepdims=True) + EPS)


def _per_head_matmul(cat, rhs, c):
    first_head = lax.broadcasted_iota(jnp.int32, (c, LANE), 1) < 64
    packed = c % (2 * SUBLANE) == 0
    if packed:
        keep_lo = jnp.where(first_head, 1.0, 0.0).astype(BF16)
        keep_hi = jnp.where(first_head, 0.0, 1.0).astype(BF16)
    cols = []
    for col in range(3):
        r = rhs[:, col * LANE:(col + 1) * LANE]
        if packed:
            rb = r.astype(BF16)
            two = jnp.concatenate([rb * keep_lo, rb * keep_hi], axis=0)
        else:
            two = jnp.concatenate([jnp.where(first_head, r, 0.0), jnp.where(first_head, 0.0, r)],
                                  axis=0).astype(BF16)
        cols.append(_dot(cat[:, 2 * col * c:(2 * col + 2) * c], two))
    return jnp.concatenate(cols, axis=1)


def _run_staged(chunk_gens, lag, fillers=None, before_start=None, after_finish=None):
    pending = list(enumerate(chunk_gens))
    active = []
    tick = 0
    while pending or active:
        if pending and tick % lag == 0:
            i, g = pending.pop(0)
            if before_start is not None:
                before_start(i)
            active.append((i, g))
        for item in list(active):
            i, g = item
            try:
                next(g)
            except StopIteration:
                active.remove(item)
                if after_finish is not None:
                    after_finish(i)
        if fillers is not None:
            next(fillers, None)
        tick += 1


MixerCfg = collections.namedtuple("MixerCfg", "C T bb nl zero_init n_alias prefetch")


def _mixer_kernel(cfg, *refs):
    C, T, bb, nl = cfg.C, cfg.T, cfg.bb, cfg.nl
    refs = list(refs)
    x_ref = refs.pop(0)
    xnext_ref = refs.pop(0) if cfg.prefetch else None
    if cfg.zero_init:
        sgla_ref = sconf_ref = ssc_ref = sssm_ref = None
    else:
        sgla_ref, sconf_ref, ssc_ref, sssm_ref = refs[:4]
        del refs[:4]
    (n1_ref, win_ref, wgk_ref, bgk_ref, gnorm_ref, cw_ref, cbias_ref, lnw_ref, lnb_ref,
     scw_ref, scb_ref, dtb_ref, alog_ref, dskip_ref, snorm_ref, wout_ref) = refs[:16]
    del refs[:16]
    (tril_ref, trilcat_ref, hm6_ref, eyecat_ref, gm6_ref, stmask_ref, htmask_ref) = refs[:7]
    del refs[:7]
    del refs[:cfg.n_alias]
    xo_ref, ogla_ref, oconf_ref, osc_ref, ossm_ref = refs[:5]
    proj_scr, xn_scr, st_scr, ht_scr, ubuf, xbuf, mix_scr = refs[5:]

    l = pl.program_id(1)
    nchunk = T // C
    ngroups = 2 if (bb == 1 and nchunk % 2 == 0) else 1
    RG = bb * T // ngroups
    NPIECE = NP // MXU_DIM

    def in_proj_full(g):
        rows = slice(g * RG, (g + 1) * RG)
        xn = _rms_scale(x_ref[rows, :]) * n1_ref[...]
        proj_scr[rows, :] = _dot(xn.astype(BF16), win_ref[...])

    def in_proj_pieces(g, src_ref=None):
        rows = slice(g * RG, (g + 1) * RG)
        src = x_ref[rows, :] if src_ref is None else src_ref[...]
        xn_scr[...] = (_rms_scale(src) * n1_ref[...]).astype(BF16)
        for p in range(NPIECE):
            cols = slice(p * MXU_DIM, (p + 1) * MXU_DIM)
            proj_scr[rows, cols] = _dot(xn_scr[...], win_ref[:, cols])
            if (p + 1) % PROJ_PIECES_PER_TICK == 0:
                yield

    def out_proj_pieces(row_lo, row_hi):
        rows = slice(row_lo, row_hi)
        for p in range(D_MODEL // MXU_DIM):
            cols = slice(p * MXU_DIM, (p + 1) * MXU_DIM)
            xo_ref[rows, cols] = x_ref[rows, cols] + _dot(mix_scr[rows, :].astype(BF16), wout_ref[:, cols])
            yield

    tril = tril_ref[...]
    trilcat = trilcat_ref[...] > 0.5

    def cumsum_rows(a):
        acc = None
        for p in _split_bf16(a, SPLIT_PARTS):
            t = _dot(tril, p)
            acc = t if acc is None else acc + t
        return acc

    first_half = lax.broadcasted_iota(jnp.int32, (C, LANE), 1) < 64

    def stack_masked(a, mask_ref, n):
        if C % (2 * SUBLANE) == 0:
            return jnp.concatenate([a.astype(BF16)] * n, axis=0) * mask_ref[...]
        return (jnp.concatenate([a] * n, axis=0) * mask_ref[...].astype(F32)).astype(BF16)

    def head_mean(a):
        cols = []
        for col in range(3):
            blk = a[:, col * LANE:(col + 1) * LANE]
            lo = jnp.sum(jnp.where(first_half, blk, 0.0), axis=-1, keepdims=True)
            hi = jnp.sum(jnp.where(first_half, 0.0, blk), axis=-1, keepdims=True)
            cols.append(jnp.where(first_half, lo, hi))
        return jnp.concatenate(cols, axis=1) * (1.0 / 64)

    def group_mean(a):
        c0, c1, c2 = (a[:, col * LANE:(col + 1) * LANE] for col in range(3))
        g0 = jnp.sum(c0 + jnp.where(first_half, c1, 0.0), axis=-1, keepdims=True)
        g1 = jnp.sum(c2 + jnp.where(first_half, 0.0, c1), axis=-1, keepdims=True)
        shape = (C, LANE)
        return jnp.concatenate([jnp.broadcast_to(g0, shape), jnp.where(first_half, g0, g1),
                                jnp.broadcast_to(g1, shape)], axis=1) * (1.0 / (SSM_HPG * SSM_HEADDIM))

    def chunk_stages(r0, c0, slot):
        def P(off, width):
            return proj_scr[pl.ds(r0, C), off:off + width]

        gk_pre = _dot(P(GLR0, LANE).astype(BF16), wgk_ref[...]) + bgk_ref[...]
        xbc = P(XBC0, SSM_XBC)
        ext = jnp.concatenate([xbuf[slot, pl.ds(c0, SUBLANE), :], xbc], axis=0)
        xbuf[slot, pl.ds(SSMC_BASE + c0, C), :] = xbc
        acc = jnp.broadcast_to(scb_ref[...], (C, SSM_XBC))
        for kk in range(SSM_CONV):
            lo = SUBLANE - (SSM_CONV - 1) + kk
            acc = acc + scw_ref[kk:kk + 1, :] * ext[lo:lo + C, :]
        xcv = _silu(acc)
        xs = xcv[:, 0:SSM_WIDTH]
        bm = xcv[:, SSM_WIDTH:SSM_WIDTH + SSM_GN]
        cm = xcv[:, SSM_WIDTH + SSM_GN:SSM_XBC]
        dtf = _softplus(P(DT0, SSM_WIDTH) + dtb_ref[...])
        af = dtf * (-jnp.exp(alog_ref[...]))
        yield

        gk = _log_sigmoid(gk_pre) * (1.0 / GLA_GATE_NORM)
        acum = cumsum_rows(af)
        b = cumsum_rows(gk)
        u = P(CA0, CONF_CH) * _sigmoid(P(CB0, CONF_CH))
        b0 = CONF_BASE + c0
        uext = jnp.concatenate([ubuf[slot, 0, pl.ds(b0 - SUBLANE, SUBLANE), :], u], axis=0)
        ubuf[slot, 0, pl.ds(b0, C), :] = u
        for s in range(1, SUBLANE):
            ubuf[slot, s, pl.ds(b0 - SUBLANE, C), :] = uext[s:s + C, :]
        xdt = xs * dtf
        bbd = stack_masked(bm, gm6_ref, SSM_HEADS)
        cb_cat = _dot_nt(cm.astype(BF16), bbd)

        def conf_taps(acc, k_lo, k_hi):
            for kk in range(k_lo, k_hi):
                s = (kk - (CONF_K - 1)) % SUBLANE
                off = kk - (CONF_K - 1) - s
                acc = acc + cw_ref[kk:kk + 1, :] * ubuf[slot, s, pl.ds(b0 + off, C), :]
            return acc
        yield

        alast = acum[C - 1:C, :]
        if C == SSM_CHUNK:
            col_cat = acum
            acum_t = acum.T
            row_cat = jnp.concatenate([acum_t[h * 64:h * 64 + 1, :] for h in range(SSM_HEADS)], axis=1)
        else:
            col_cat = jnp.concatenate([acum[:, h * 64:h * 64 + C] for h in range(SSM_HEADS)], axis=1)
            ones_cc = jnp.ones((C, C), BF16)
            row_cat = None
            for p in _split_bf16(col_cat * eyecat_ref[...], SPLIT_PARTS):
                t = _dot(ones_cc, p)
                row_cat = t if row_cat is None else row_cat + t
        lmat = jnp.exp(jnp.where(trilcat, col_cat - row_cat, -jnp.inf))
        blast = b[C - 1:C, :]
        q_in = P(Q0, QK_PAD) * (GLA_DK ** -0.5) * jnp.exp(b)
        k = P(K0, QK_PAD)
        k_in = k * jnp.exp(-b)
        k_out = k * jnp.exp(blast - b)
        v = P(V0, GLA_WIDTH)
        vb = v.astype(BF16)
        q_inb = q_in.astype(BF16)
        kbd = stack_masked(k_in, hm6_ref, GLA_HEADS)
        att = _dot_nt(q_inb, kbd)
        cacc = conf_taps(jnp.broadcast_to(cbias_ref[...], (C, CONF_CH)), 0, 10)
        yield

        att = jnp.where(trilcat, att, 0.0)
        o_intra = _per_head_matmul(att.astype(BF16), v, C)
        y_intra = _per_head_matmul((cb_cat * lmat).astype(BF16), xdt, C)
        k_outb = k_out.astype(BF16)
        decay = jnp.exp(blast)
        o_inter = []
        for r_lo, r_hi, l_lo in ((0, 4 * GLA_DV, 0), (4 * GLA_DV, GLA_WIDTH, LANE)):
            st = st_scr[slot, r_lo:r_hi, l_lo:l_lo + LANE]
            o_inter.append(_dot_nt(q_inb[:, l_lo:l_lo + LANE], st.astype(BF16)))
            kv = _dot_tn(vb[:, r_lo:r_hi], k_outb[:, l_lo:l_lo + LANE])
            st_scr[slot, r_lo:r_hi, l_lo:l_lo + LANE] = (
                st * decay[:, l_lo:l_lo + LANE] + kv * stmask_ref[r_lo:r_hi, l_lo:l_lo + LANE])
        o = o_intra + jnp.concatenate(o_inter, axis=1)
        ht = ht_scr[slot]
        y_inter = _dot(cm.astype(BF16), ht.astype(BF16)) * jnp.exp(acum)
        cacc = conf_taps(cacc, 10, 20)
        wx = xdt * jnp.exp(alast - acum)
        stn = _dot_tn(bm.astype(BF16), wx.astype(BF16))
        ht_scr[slot] = ht * jnp.exp(alast) + stn * htmask_ref[...]
        yield

        ms_o = head_mean(o * o)
        y = y_intra + y_inter + dskip_ref[...] * xs
        y = y * _silu(P(Z0, SSM_WIDTH))
        ms_y = group_mean(y * y)
        cacc = conf_taps(cacc, 20, CONF_K)
        mu = jnp.mean(cacc, axis=-1, keepdims=True)
        xc = cacc - mu
        var = jnp.mean(xc * xc, axis=-1, keepdims=True)
        cval = _silu(xc * lax.rsqrt(var + LN_EPS) * lnw_ref[...] + lnb_ref[...])
        mix_scr[pl.ds(r0, C), GLA_WIDTH:GLA_WIDTH + CONF_CH] = cval.astype(mix_scr.dtype)
        o = o * lax.rsqrt(ms_o + EPS) * gnorm_ref[...] * _silu(P(GO0, GLA_WIDTH))
        mix_scr[pl.ds(r0, C), 0:GLA_WIDTH] = o.astype(mix_scr.dtype)
        y = y * lax.rsqrt(ms_y + EPS) * snorm_ref[...]
        mix_scr[pl.ds(r0, C), GLA_WIDTH + CONF_CH:D_MODEL] = y.astype(mix_scr.dtype)

    def load_conv_hist(slot, hconf, hsc):
        hist = jnp.concatenate([jnp.zeros((CONF_BASE - (CONF_K - 1), CONF_CH), F32), hconf], axis=0)
        ubuf[slot, 0, 0:CONF_BASE, :] = hist
        for s in range(1, SUBLANE):
            ubuf[slot, s, 0:CONF_BASE - SUBLANE, :] = hist[s:s + CONF_BASE - SUBLANE, :]
        xbuf[slot, 0:SSMC_BASE, :] = jnp.concatenate(
            [jnp.zeros((SSMC_BASE - (SSM_CONV - 1), SSM_XBC), F32), hsc], axis=0)

    def load_state(j, slot, first):
        if not first:
            load_conv_hist(slot, oconf_ref[j], osc_ref[j])
        elif cfg.zero_init:
            st_scr[slot] = jnp.zeros((GLA_WIDTH, QK_PAD), F32)
            ht_scr[slot] = jnp.zeros((SSM_GN, SSM_WIDTH), F32)
            load_conv_hist(slot, jnp.zeros((CONF_K - 1, CONF_CH), F32), jnp.zeros((SSM_CONV - 1, SSM_XBC), F32))
        else:
            rows = []
            for h in range(GLA_HEADS):
                parts = []
                if h > 0:
                    parts.append(jnp.zeros((GLA_DK, h * GLA_DV), F32))
                parts.append(sgla_ref[j, h])
                if h < GLA_HEADS - 1:
                    parts.append(jnp.zeros((GLA_DK, (GLA_HEADS - 1 - h) * GLA_DV), F32))
                rows.append(jnp.concatenate(parts, axis=1))
            rows.append(jnp.zeros((QK_PAD - GLA_QK, GLA_WIDTH), F32))
            st_scr[slot] = jnp.concatenate(rows, axis=0).T
            rows = []
            for h in range(SSM_HEADS):
                z = jnp.zeros((SSM_HEADDIM, SSM_STATE), F32)
                rows.append(jnp.concatenate([sssm_ref[j, h], z] if h // SSM_HPG == 0 else [z, sssm_ref[j, h]], axis=1))
            ht_scr[slot] = jnp.concatenate(rows, axis=0).T
            load_conv_hist(slot, sconf_ref[j], ssc_ref[j])

    def store_conv_state(j, slot):
        oconf_ref[j] = ubuf[slot, 0, T + CONF_BASE - (CONF_K - 1):T + CONF_BASE, :]
        osc_ref[j] = xbuf[slot, T + SSMC_BASE - (SSM_CONV - 1):T + SSMC_BASE, :]

    def store_state(j, slot):
        stt = st_scr[slot].T
        for h in range(GLA_HEADS):
            ogla_ref[j, h] = stt[h * GLA_DK:(h + 1) * GLA_DK, h * GLA_DV:(h + 1) * GLA_DV]
        htt = ht_scr[slot].T
        for h in range(SSM_HEADS):
            g = h // SSM_HPG
            ossm_ref[j, h] = htt[h * SSM_HEADDIM:(h + 1) * SSM_HEADDIM, g * SSM_STATE:(g + 1) * SSM_STATE]

    if bb == 1:
        if cfg.prefetch:
            @pl.when((pl.program_id(0) == 0) & (l == 0))
            def _():
                in_proj_full(0)
        else:
            in_proj_full(0)
        if nl == 1:
            load_state(0, 0, True)
        else:
            @pl.when(l == 0)
            def _():
                load_state(0, 0, True)

            @pl.when(l > 0)
            def _():
                load_state(0, 0, False)

        cpg = nchunk // ngroups
        work = collections.deque()
        if ngroups == 2:
            work.append(in_proj_pieces(1))

        def drain_work():
            while work:
                for _ in work.popleft():
                    pass

        def before_start(i):
            if ngroups == 2 and i == cpg:
                drain_work()

        def after_finish(i):
            if (i + 1) % OUT_PROJ_CHUNKS == 0 or i == nchunk - 1:
                lo = (i // OUT_PROJ_CHUNKS) * OUT_PROJ_CHUNKS
                work.append(out_proj_pieces(lo * C, (i + 1) * C))
            if cfg.prefetch and i == cpg - 1:
                work.append(in_proj_pieces(0, xnext_ref))

        def work_pieces():
            while True:
                if work:
                    try:
                        next(work[0])
                    except StopIteration:
                        work.popleft()
                yield

        _run_staged([chunk_stages(ci * C, ci * C, 0) for ci in range(nchunk)], CHUNK_STAGE_LAG,
                    fillers=work_pieces(), before_start=before_start, after_finish=after_finish)
        drain_work()

        store_conv_state(0, 0)
        if nl == 1:
            store_state(0, 0)
        else:
            @pl.when(l == nl - 1)
            def _():
                store_state(0, 0)
    else:
        assert nl == 1 and nchunk == 1
        in_proj_full(0)

        def seq_stages(j):
            slot = j % SEQ_SLOTS
            load_state(j, slot, True)
            yield
            yield from chunk_stages(j * T, 0, slot)
            store_conv_state(j, slot)
            store_state(j, slot)

        _run_staged([seq_stages(j) for j in range(bb)], SEQ_STAGE_LAG)
        xo_ref[...] = x_ref[...] + _dot(mix_scr[...].astype(BF16), wout_ref[...])


def _np_consts(C):
    i = np.arange(C)
    tril = (i[None, :] <= i[:, None]).astype(np.float32)
    trilcat = np.tile(tril, (1, 6))
    eyecat = np.tile(np.eye(C, dtype=np.float32), (1, 6))
    hm6 = np.zeros((6 * C, QK_PAD), np.float32)
    gm6 = np.zeros((6 * C, SSM_GN), np.float32)
    for h in range(6):
        hm6[h * C:(h + 1) * C, h * GLA_DK:(h + 1) * GLA_DK] = 1.0
        g = h // SSM_HPG
        gm6[h * C:(h + 1) * C, g * SSM_STATE:(g + 1) * SSM_STATE] = 1.0
    stmask = np.zeros((GLA_WIDTH, QK_PAD), np.float32)
    htmask = np.zeros((SSM_GN, SSM_WIDTH), np.float32)
    for h in range(6):
        stmask[h * 64:(h + 1) * 64, h * 32:(h + 1) * 32] = 1.0
        g = h // SSM_HPG
        htmask[g * 64:(g + 1) * 64, h * 64:(h + 1) * 64] = 1.0
    return (jnp.asarray(tril, BF16), jnp.asarray(trilcat, F32), jnp.asarray(hm6, BF16), jnp.asarray(eyecat, F32),
            jnp.asarray(gm6, BF16), jnp.asarray(stmask, F32), jnp.asarray(htmask, F32))


def _const_spec(a):
    nd = a.ndim
    return pl.BlockSpec(a.shape, lambda i, l: (0,) * nd, pipeline_mode=pl.Buffered(1))


def _layer_spec(a, layer, block=None, col=0):
    shape = a.shape[1:] if block is None else block
    return pl.BlockSpec((None,) + tuple(shape), lambda i, l: (layer, 0, col), pipeline_mode=pl.Buffered(1))


def _state_spec(a, layer, bb):
    nrest = a.ndim - 2
    return pl.BlockSpec((None, bb) + tuple(a.shape[2:]), lambda i, l: (layer, i) + (0,) * nrest)


def _mixer_call(x2d, states_in, states_prev, wts, layer, Bn, L, bb, T, C):
    assert L % T == 0 and T % C == 0 and Bn % bb == 0 and C % SUBLANE == 0
    nl = L // T
    assert bb == 1 or nl == 1
    nb = Bn // bb
    R = bb * T
    nslot = 1 if bb == 1 else SEQ_SLOTS
    consts = _np_consts(C)
    x_spec = pl.BlockSpec((R, D_MODEL), lambda i, l: (i * nl + l, 0))
    state_shapes = ((DEPTH, Bn, GLA_HEADS, GLA_DK, GLA_DV), (DEPTH, Bn, CONF_K - 1, CONF_CH),
                    (DEPTH, Bn, SSM_CONV - 1, SSM_XBC), (DEPTH, Bn, SSM_HEADS, SSM_HEADDIM, SSM_STATE))
    out_shape = ([jax.ShapeDtypeStruct((Bn * L, D_MODEL), F32)]
                 + [jax.ShapeDtypeStruct(s, F32) for s in state_shapes])
    out_specs = [x_spec] + [_state_spec(s, layer, bb) for s in out_shape[1:]]

    ngroups = 2 if (bb == 1 and (T // C) % 2 == 0) else 1
    prefetch = ngroups == 2
    operands = [x2d]
    in_specs = [x_spec]
    if prefetch:
        rg = R // ngroups
        last_block = (Bn * L) // rg - ngroups
        operands.append(x2d)
        in_specs.append(pl.BlockSpec(
            (rg, D_MODEL), lambda i, l: (jnp.minimum((i * nl + l + 1) * ngroups, last_block), 0)))
    if states_in is not None:
        operands += list(states_in)
        in_specs += [_state_spec(a, layer, bb) for a in states_in]
    operands += list(wts)
    in_specs += [_layer_spec(a, layer) for a in wts]
    operands += list(consts)
    in_specs += [_const_spec(a) for a in consts]
    aliases = {}
    if states_prev is not None:
        for k, a in enumerate(states_prev):
            aliases[len(operands)] = 1 + k
            operands.append(a)
            in_specs.append(pl.BlockSpec(memory_space=pl.ANY))
    cfg = MixerCfg(C, T, bb, nl, states_in is None, len(aliases), prefetch)
    scratch = [
        pltpu.VMEM((R, NP), F32),
        pltpu.VMEM((R // ngroups, D_MODEL), BF16),
        pltpu.VMEM((nslot, GLA_WIDTH, QK_PAD), F32),
        pltpu.VMEM((nslot, SSM_GN, SSM_WIDTH), F32),
        pltpu.VMEM((nslot, SUBLANE, CONF_BASE + T, CONF_CH), F32),
        pltpu.VMEM((nslot, SSMC_BASE + T, SSM_XBC), F32),
        pltpu.VMEM((R, D_MODEL), BF16 if C % (2 * SUBLANE) == 0 else F32),
    ]
    return pl.pallas_call(
        functools.partial(_mixer_kernel, cfg),
        grid=(nb, nl),
        in_specs=in_specs,
        out_specs=out_specs,
        out_shape=out_shape,
        scratch_shapes=scratch,
        input_output_aliases=aliases,
        compiler_params=pltpu.CompilerParams(
            dimension_semantics=("arbitrary", "arbitrary"), vmem_limit_bytes=VMEM_LIMIT_BYTES),
        name="mixer",
    )(*operands)


FfnCfg = collections.namedtuple("FfnCfg", "T bb nl final zero_init n_alias")


def _ffn_kernel(cfg, *refs):
    T, bb, nl = cfg.T, cfg.bb, cfg.nl
    refs = list(refs)
    x_ref = refs.pop(0)
    sffn_ref = None if cfg.zero_init else refs.pop(0)
    n2_ref, wa_ref, wg_ref, fcw_ref, fcb_ref, wo_ref, fn_ref = refs[:7]
    del refs[:7]
    del refs[:cfg.n_alias]
    xo_ref, offn_ref, abuf, act = refs

    l = pl.program_id(1)
    R = bb * T
    x = x_ref[...]
    xn = (_rms_scale(x) * n2_ref[...]).astype(BF16)
    lo = FFNC_BASE - (FFN_CONV - 1)

    def init_hist():
        if cfg.zero_init:
            abuf[:, lo:FFNC_BASE, :] = jnp.zeros((bb, FFN_CONV - 1, D_FF), F32)
        else:
            abuf[:, lo:FFNC_BASE, :] = sffn_ref[...]

    if nl == 1:
        init_hist()
    else:
        @pl.when(l == 0)
        def _():
            init_hist()

        @pl.when(l > 0)
        def _():
            abuf[:, lo:FFNC_BASE, :] = offn_ref[...]
    f0 = 0
    for fw in FFN_SPLITS:
        a = _dot(xn, wa_ref[:, f0:f0 + fw])
        g = _dot(xn, wg_ref[:, f0:f0 + fw])
        abuf[:, FFNC_BASE:FFNC_BASE + T, f0:f0 + fw] = a.reshape(bb, T, fw)
        ac = fcb_ref[:, f0:f0 + fw][None]
        for kk in range(FFN_CONV):
            ac = ac + fcw_ref[kk:kk + 1, f0:f0 + fw][None] * abuf[:, lo + kk:lo + kk + T, f0:f0 + fw]
        offn_ref[:, :, f0:f0 + fw] = abuf[:, T + lo:T + FFNC_BASE, f0:f0 + fw]
        act[:, f0:f0 + fw] = (_silu(ac).reshape(R, fw) * g).astype(BF16)
        f0 += fw
    y = x + _dot(act[...], wo_ref[...])
    if cfg.final:
        y = _rms_scale(y) * fn_ref[...]
    xo_ref[...] = y


def _ffn_call(x2d, state_in, state_prev, wts, layer, Bn, L, bb, T, final):
    assert L % T == 0 and Bn % bb == 0
    nl = L // T
    assert bb == 1 or nl == 1
    nb = Bn // bb
    R = bb * T
    n2, w_ffn_in, fcw, fcb, w_ffn_out, fn = wts
    x_spec = pl.BlockSpec((R, D_MODEL), lambda i, l: (i * nl + l, 0))
    st_shape = jax.ShapeDtypeStruct((DEPTH, Bn, FFN_CONV - 1, D_FF), F32)
    operands = [x2d]
    in_specs = [x_spec]
    if state_in is not None:
        operands.append(state_in)
        in_specs.append(_state_spec(state_in, layer, bb))
    operands += [n2, w_ffn_in, w_ffn_in, fcw, fcb, w_ffn_out, fn]
    in_specs += [_layer_spec(n2, layer),
                 _layer_spec(w_ffn_in, layer, block=(D_MODEL, D_FF), col=0),
                 _layer_spec(w_ffn_in, layer, block=(D_MODEL, D_FF), col=1),
                 _layer_spec(fcw, layer), _layer_spec(fcb, layer), _layer_spec(w_ffn_out, layer), _const_spec(fn)]
    aliases = {}
    if state_prev is not None:
        aliases[len(operands)] = 1
        operands.append(state_prev)
        in_specs.append(pl.BlockSpec(memory_space=pl.ANY))
    cfg = FfnCfg(T, bb, nl, final, state_in is None, len(aliases))
    return pl.pallas_call(
        functools.partial(_ffn_kernel, cfg),
        grid=(nb, nl),
        in_specs=in_specs,
        out_specs=[x_spec, _state_spec(st_shape, layer, bb)],
        out_shape=[jax.ShapeDtypeStruct((Bn * L, D_MODEL), F32), st_shape],
        scratch_shapes=[pltpu.VMEM((bb, FFNC_BASE + T, D_FF), F32),
                        pltpu.VMEM((R, D_FF), BF16)],
        input_output_aliases=aliases,
        compiler_params=pltpu.CompilerParams(
            dimension_semantics=("arbitrary", "arbitrary"), vmem_limit_bytes=VMEM_LIMIT_BYTES),
        name="ffn",
    )(*operands)


def _prep_weights(norm1, w_in, w_gk, b_gk, gla_norm, conf_w, conf_b, conf_ln_w, conf_ln_b,
                  ssm_conv_w, ssm_conv_b, dt_bias, a_log, d_skip, ssm_norm, w_out,
                  norm2, w_ffn_in, ffn_conv_w, ffn_conv_b, w_ffn_out, final_norm):
    offs = np.concatenate([[0], np.cumsum(IN_SIZES)])
    wb = w_in.astype(BF16)
    seg = lambda i: wb[:, :, int(offs[i]):int(offs[i + 1])]
    zpad = lambda n: jnp.zeros((DEPTH, D_MODEL, n), BF16)
    w_in_p = jnp.concatenate([
        seg(0), zpad(QK_PAD - GLA_QK), seg(1), zpad(QK_PAD - GLA_QK), seg(2), seg(3),
        seg(5), seg(6), seg(7), seg(8), jnp.repeat(seg(9), SSM_HEADDIM, axis=2),
        seg(4), zpad(LANE - GLA_RANK)], axis=2)
    assert w_in_p.shape == (DEPTH, D_MODEL, NP)
    row = lambda v: v.reshape(DEPTH, 1, -1).astype(F32)
    wgk_p = jnp.zeros((DEPTH, LANE, QK_PAD), F32).at[:, :GLA_RANK, :GLA_QK].set(w_gk).astype(BF16)
    bgk_p = jnp.zeros((DEPTH, 1, QK_PAD), F32).at[:, 0, :GLA_QK].set(b_gk)
    rep64 = lambda v: row(jnp.repeat(v, SSM_HEADDIM, axis=1))
    mixer_w = (row(norm1), w_in_p, wgk_p, bgk_p, row(jnp.tile(gla_norm, (1, GLA_HEADS))),
               conf_w.astype(F32), row(conf_b), row(conf_ln_w), row(conf_ln_b),
               ssm_conv_w.astype(F32), row(ssm_conv_b), rep64(dt_bias), rep64(a_log),
               rep64(d_skip), row(ssm_norm), w_out.astype(BF16))
    ffn_w = (row(norm2), w_ffn_in.astype(BF16), ffn_conv_w.astype(F32), row(ffn_conv_b),
             w_ffn_out.astype(BF16), final_norm.reshape(1, -1).astype(F32))
    return mixer_w, ffn_w


def _run_group(x, states, mixer_w, ffn_w, bb, T, C, ffn_bb=None, ffn_T=None):
    ffn_bb = bb if ffn_bb is None else ffn_bb
    ffn_T = T if ffn_T is None else ffn_T
    Bn, L, _ = x.shape
    x2d = x.reshape(Bn * L, D_MODEL)
    out_states = None
    for layer in range(DEPTH):
        prev4 = None if out_states is None else out_states[:4]
        prev_ffn = None if out_states is None else out_states[4]
        x2d, ngla, nconf, nsc, nssm = _mixer_call(
            x2d, None if states is None else states[:4], prev4, mixer_w, layer, Bn, L, bb, T, C)
        x2d, nffn = _ffn_call(x2d, None if states is None else states[4], prev_ffn, ffn_w, layer, Bn, L,
                              ffn_bb, ffn_T, final=(layer == DEPTH - 1))
        out_states = (ngla, nconf, nsc, nssm, nffn)
    return x2d.reshape(Bn, L, D_MODEL), out_states


FFN_ROWS = 512


def _tiles(Lp, Bs, Ls):
    tp = min(512, Lp)
    return tp, math.gcd(tp, GLA_CHUNK), min(16, Bs), math.gcd(Ls, GLA_CHUNK), min(max(FFN_ROWS // Ls, 1), Bs)


def kernel(x_prompt, x_sample, state_gla, state_conf_conv, state_ssm_conv, state_ssm, state_ffn_conv, norm1, w_in, w_gk, b_gk, gla_norm, conf_w, conf_b, conf_ln_w, conf_ln_b, ssm_conv_w, ssm_conv_b, dt_bias, a_log, d_skip, ssm_norm, w_out, norm2, w_ffn_in, ffn_conv_w, ffn_conv_b, w_ffn_out, final_norm):
    mixer_w, ffn_w = _prep_weights(norm1, w_in, w_gk, b_gk, gla_norm, conf_w, conf_b, conf_ln_w, conf_ln_b,
                                   ssm_conv_w, ssm_conv_b, dt_bias, a_log, d_skip, ssm_norm, w_out,
                                   norm2, w_ffn_in, ffn_conv_w, ffn_conv_b, w_ffn_out, final_norm)
    tp, cp, bbs, cs, bbf = _tiles(x_prompt.shape[1], x_sample.shape[0], x_sample.shape[1])
    y_p, st_p = _run_group(x_prompt, None, mixer_w, ffn_w, 1, tp, cp)
    s_states = (state_gla, state_conf_conv, state_ssm_conv, state_ssm, state_ffn_conv)
    y_s, st_s = _run_group(x_sample, s_states, mixer_w, ffn_w, bbs, x_sample.shape[1], cs, ffn_bb=bbf)
    return (y_p, y_s) + tuple(st_p) + tuple(st_s)
```

```python
import collections
import functools
import math

import numpy as np
import jax
import jax.numpy as jnp
from jax import lax
from jax.experimental import pallas as pl
from jax.experimental.pallas import tpu as pltpu

F32 = jnp.float32
BF16 = jnp.bfloat16

D_MODEL = 1024
DEPTH = 2
GLA_HEADS = 6
GLA_DK = 32
GLA_DV = 64
GLA_QK = GLA_HEADS * GLA_DK
GLA_WIDTH = GLA_HEADS * GLA_DV
GLA_RANK = 16
GLA_GATE_NORM = 16.0
GLA_CHUNK = 64
CONF_CH = 256
CONF_K = 31
LN_EPS = 1e-5
SSM_WIDTH = 384
SSM_HEADDIM = 64
SSM_HEADS = 6
SSM_GROUPS = 2
SSM_HPG = 3
SSM_STATE = 64
SSM_GN = SSM_GROUPS * SSM_STATE
SSM_XBC = SSM_WIDTH + 2 * SSM_GN
SSM_CONV = 4
SSM_CHUNK = 64
D_FF = 2816
FFN_CONV = 3
EPS = 1e-6
IN_SIZES = (GLA_QK, GLA_QK, GLA_WIDTH, GLA_WIDTH, GLA_RANK, CONF_CH, CONF_CH, SSM_WIDTH, SSM_XBC, SSM_HEADS)

LANE = 128
SUBLANE = 8
MXU_DIM = 256
VMEM_LIMIT_BYTES = 60 * 1024 * 1024

QK_PAD = 256
Q0 = 0
K0 = Q0 + QK_PAD
V0 = K0 + QK_PAD
GO0 = V0 + GLA_WIDTH
CA0 = GO0 + GLA_WIDTH
CB0 = CA0 + CONF_CH
Z0 = CB0 + CONF_CH
XBC0 = Z0 + SSM_WIDTH
DT0 = XBC0 + SSM_XBC
GLR0 = DT0 + SSM_WIDTH
NP = GLR0 + LANE

SPLIT_PARTS = 2

FFN_SPLITS = (768, 768, 768, 512)

CONF_BASE = 32
SSMC_BASE = 8
FFNC_BASE = 8

CHUNK_STAGE_LAG = 3
SEQ_STAGES = 6
SEQ_STAGE_LAG = 1
SEQ_SLOTS = -(-SEQ_STAGES // SEQ_STAGE_LAG)
PROJ_PIECES_PER_TICK = 2


def _dot(a, b):
    return jnp.dot(a, b, preferred_element_type=F32)


def _dot_nt(a, b):
    return lax.dot_general(a, b, (((1,), (1,)), ((), ())), preferred_element_type=F32)


def _dot_tn(a, b):
    return lax.dot_general(a, b, (((0,), (0,)), ((), ())), preferred_element_type=F32)


def _split_bf16(a, parts):
    out = []
    r = a
    for i in range(parts):
        p = r.astype(BF16)
        out.append(p)
        if i + 1 < parts:
            r = r - p.astype(F32)
    return out


def _sigmoid(x):
    return 1.0 / (1.0 + jnp.exp(-x))


def _silu(x):
    return x * _sigmoid(x)


def _softplus(x):
    return jnp.maximum(x, 0.0) + jnp.log(1.0 + jnp.exp(-jnp.abs(x)))


def _log_sigmoid(x):
    return jnp.minimum(x, 0.0) - jnp.log(1.0 + jnp.exp(-jnp.abs(x)))


def _rms_scale(x):
    return x * lax.rsqrt(jnp.mean(x * x, axis=-1, keepdims=True) + EPS)


def _per_head_matmul(cat, rhs, c):
    first_head = lax.broadcasted_iota(jnp.int32, (c, LANE), 1) < 64
    packed = c % (2 * SUBLANE) == 0
    if packed:
        keep_lo = jnp.where(first_head, 1.0, 0.0).astype(BF16)
        keep_hi = jnp.where(first_head, 0.0, 1.0).astype(BF16)
    cols = []
    for col in range(3):
        r = rhs[:, col * LANE:(col + 1) * LANE]
        if packed:
            rb = r.astype(BF16)
            two = jnp.concatenate([rb * keep_lo, rb * keep_hi], axis=0)
        else:
            two = jnp.concatenate([jnp.where(first_head, r, 0.0), jnp.where(first_head, 0.0, r)],
                                  axis=0).astype(BF16)
        cols.append(_dot(cat[:, 2 * col * c:(2 * col + 2) * c], two))
    return jnp.concatenate(cols, axis=1)


def _run_staged(chunk_gens, lag, fillers=None, before_start=None, after_finish=None):
    pending = list(enumerate(chunk_gens))
    active = []
    tick = 0
    while pending or active:
        if pending and tick % lag == 0:
            i, g = pending.pop(0)
            if before_start is not None:
                before_start(i)
            active.append((i, g))
        for item in list(active):
            i, g = item
            try:
                next(g)
            except StopIteration:
                active.remove(item)
                if after_finish is not None:
                    after_finish(i)
        if fillers is not None:
            next(fillers, None)
        tick += 1


MixerCfg = collections.namedtuple("MixerCfg", "C T bb nl zero_init n_alias prefetch")


def _mixer_kernel(cfg, *refs):
    C, T, bb, nl = cfg.C, cfg.T, cfg.bb, cfg.nl
    refs = list(refs)
    x_ref = refs.pop(0)
    xnext_ref = refs.pop(0) if cfg.prefetch else None
    if cfg.zero_init:
        sgla_ref = sconf_ref = ssc_ref = sssm_ref = None
    else:
        sgla_ref, sconf_ref, ssc_ref, sssm_ref = refs[:4]
        del refs[:4]
    (n1_ref, win_ref, wgk_ref, bgk_ref, gnorm_ref, cw_ref, cbias_ref, lnw_ref, lnb_ref,
     scw_ref, scb_ref, dtb_ref, alog_ref, dskip_ref, snorm_ref, wout_ref) = refs[:16]
    del refs[:16]
    (tril_ref, trilcat_ref, hm6_ref, eyecat_ref, gm6_ref, stmask_ref, htmask_ref) = refs[:7]
    del refs[:7]
    del refs[:cfg.n_alias]
    xo_ref, ogla_ref, oconf_ref, osc_ref, ossm_ref = refs[:5]
    proj_scr, xn_scr, st_scr, ht_scr, ubuf, xbuf, mix_scr = refs[5:]

    l = pl.program_id(1)
    nchunk = T // C
    ngroups = 2 if (bb == 1 and nchunk % 2 == 0) else 1
    RG = bb * T // ngroups
    NPIECE = NP // MXU_DIM

    def in_proj_full(g):
        rows = slice(g * RG, (g + 1) * RG)
        xn = _rms_scale(x_ref[rows, :]) * n1_ref[...]
        proj_scr[rows, :] = _dot(xn.astype(BF16), win_ref[...])

    def in_proj_pieces(g, src_ref=None):
        rows = slice(g * RG, (g + 1) * RG)
        src = x_ref[rows, :] if src_ref is None else src_ref[...]
        xn_scr[...] = (_rms_scale(src) * n1_ref[...]).astype(BF16)
        for p in range(NPIECE):
            cols = slice(p * MXU_DIM, (p + 1) * MXU_DIM)
            proj_scr[rows, cols] = _dot(xn_scr[...], win_ref[:, cols])
            if (p + 1) % PROJ_PIECES_PER_TICK == 0:
                yield

    def out_proj_pieces(g):
        rows = slice(g * RG, (g + 1) * RG)
        for p in range(D_MODEL // MXU_DIM):
            cols = slice(p * MXU_DIM, (p + 1) * MXU_DIM)
            xo_ref[rows, cols] = x_ref[rows, cols] + _dot(mix_scr[rows, :].astype(BF16), wout_ref[:, cols])
            yield

    tril = tril_ref[...]
    trilcat = trilcat_ref[...] > 0.5

    def cumsum_rows(a):
        acc = None
        for p in _split_bf16(a, SPLIT_PARTS):
            t = _dot(tril, p)
            acc = t if acc is None else acc + t
        return acc

    first_half = lax.broadcasted_iota(jnp.int32, (C, LANE), 1) < 64

    def stack_masked(a, mask_ref, n):
        if C % (2 * SUBLANE) == 0:
            return jnp.concatenate([a.astype(BF16)] * n, axis=0) * mask_ref[...]
        return (jnp.concatenate([a] * n, axis=0) * mask_ref[...].astype(F32)).astype(BF16)

    def head_mean(a):
        cols = []
        for col in range(3):
            blk = a[:, col * LANE:(col + 1) * LANE]
            lo = jnp.sum(jnp.where(first_half, blk, 0.0), axis=-1, keepdims=True)
            hi = jnp.sum(jnp.where(first_half, 0.0, blk), axis=-1, keepdims=True)
            cols.append(jnp.where(first_half, lo, hi))
        return jnp.concatenate(cols, axis=1) * (1.0 / 64)

    def group_mean(a):
        c0, c1, c2 = (a[:, col * LANE:(col + 1) * LANE] for col in range(3))
        g0 = jnp.sum(c0 + jnp.where(first_half, c1, 0.0), axis=-1, keepdims=True)
        g1 = jnp.sum(c2 + jnp.where(first_half, 0.0, c1), axis=-1, keepdims=True)
        shape = (C, LANE)
        return jnp.concatenate([jnp.broadcast_to(g0, shape), jnp.where(first_half, g0, g1),
                                jnp.broadcast_to(g1, shape)], axis=1) * (1.0 / (SSM_HPG * SSM_HEADDIM))

    def chunk_stages(r0, c0, slot):
        def P(off, width):
            return proj_scr[pl.ds(r0, C), off:off + width]

        gk_pre = _dot(P(GLR0, LANE).astype(BF16), wgk_ref[...]) + bgk_ref[...]
        xbc = P(XBC0, SSM_XBC)
        ext = jnp.concatenate([xbuf[slot, pl.ds(c0, SUBLANE), :], xbc], axis=0)
        xbuf[slot, pl.ds(SSMC_BASE + c0, C), :] = xbc
        acc = jnp.broadcast_to(scb_ref[...], (C, SSM_XBC))
        for kk in range(SSM_CONV):
            lo = SUBLANE - (SSM_CONV - 1) + kk
            acc = acc + scw_ref[kk:kk + 1, :] * ext[lo:lo + C, :]
        xcv = _silu(acc)
        xs = xcv[:, 0:SSM_WIDTH]
        bm = xcv[:, SSM_WIDTH:SSM_WIDTH + SSM_GN]
        cm = xcv[:, SSM_WIDTH + SSM_GN:SSM_XBC]
        dtf = _softplus(P(DT0, SSM_WIDTH) + dtb_ref[...])
        af = dtf * (-jnp.exp(alog_ref[...]))
        yield

        gk = _log_sigmoid(gk_pre) * (1.0 / GLA_GATE_NORM)
        acum = cumsum_rows(af)
        b = cumsum_rows(gk)
        u = P(CA0, CONF_CH) * _sigmoid(P(CB0, CONF_CH))
        b0 = CONF_BASE + c0
        uext = jnp.concatenate([ubuf[slot, 0, pl.ds(b0 - SUBLANE, SUBLANE), :], u], axis=0)
        ubuf[slot, 0, pl.ds(b0, C), :] = u
        for s in range(1, SUBLANE):
            ubuf[slot, s, pl.ds(b0 - SUBLANE, C), :] = uext[s:s + C, :]
        xdt = xs * dtf
        bbd = stack_masked(bm, gm6_ref, SSM_HEADS)
        cb_cat = _dot_nt(cm.astype(BF16), bbd)

        def conf_taps(acc, k_lo, k_hi):
            for kk in range(k_lo, k_hi):
                s = (kk - (CONF_K - 1)) % SUBLANE
                off = kk - (CONF_K - 1) - s
                acc = acc + cw_ref[kk:kk + 1, :] * ubuf[slot, s, pl.ds(b0 + off, C), :]
            return acc
        yield

        alast = acum[C - 1:C, :]
        if C == SSM_CHUNK:
            col_cat = acum
            acum_t = acum.T
            row_cat = jnp.concatenate([acum_t[h * 64:h * 64 + 1, :] for h in range(SSM_HEADS)], axis=1)
        else:
            col_cat = jnp.concatenate([acum[:, h * 64:h * 64 + C] for h in range(SSM_HEADS)], axis=1)
            ones_cc = jnp.ones((C, C), BF16)
            row_cat = None
            for p in _split_bf16(col_cat * eyecat_ref[...], SPLIT_PARTS):
                t = _dot(ones_cc, p)
                row_cat = t if row_cat is None else row_cat + t
        lmat = jnp.exp(jnp.where(trilcat, col_cat - row_cat, -jnp.inf))
        blast = b[C - 1:C, :]
        q_in = P(Q0, QK_PAD) * (GLA_DK ** -0.5) * jnp.exp(b)
        k = P(K0, QK_PAD)
        k_in = k * jnp.exp(-b)
        k_out = k * jnp.exp(blast - b)
        v = P(V0, GLA_WIDTH)
        vb = v.astype(BF16)
        q_inb = q_in.astype(BF16)
        kbd = stack_masked(k_in, hm6_ref, GLA_HEADS)
        att = _dot_nt(q_inb, kbd)
        cacc = conf_taps(jnp.broadcast_to(cbias_ref[...], (C, CONF_CH)), 0, 10)
        yield

        att = jnp.where(trilcat, att, 0.0)
        o_intra = _per_head_matmul(att.astype(BF16), v, C)
        y_intra = _per_head_matmul((cb_cat * lmat).astype(BF16), xdt, C)
        k_outb = k_out.astype(BF16)
        decay = jnp.exp(blast)
        o_inter = []
        for r_lo, r_hi, l_lo in ((0, 4 * GLA_DV, 0), (4 * GLA_DV, GLA_WIDTH, LANE)):
            st = st_scr[slot, r_lo:r_hi, l_lo:l_lo + LANE]
            o_inter.append(_dot_nt(q_inb[:, l_lo:l_lo + LANE], st.astype(BF16)))
            kv = _dot_tn(vb[:, r_lo:r_hi], k_outb[:, l_lo:l_lo + LANE])
            st_scr[slot, r_lo:r_hi, l_lo:l_lo + LANE] = (
                st * decay[:, l_lo:l_lo + LANE] + kv * stmask_ref[r_lo:r_hi, l_lo:l_lo + LANE])
        o = o_intra + jnp.concatenate(o_inter, axis=1)
        ht = ht_scr[slot]
        y_inter = _dot(cm.astype(BF16), ht.astype(BF16)) * jnp.exp(acum)
        cacc = conf_taps(cacc, 10, 20)
        wx = xdt * jnp.exp(alast - acum)
        stn = _dot_tn(bm.astype(BF16), wx.astype(BF16))
        ht_scr[slot] = ht * jnp.exp(alast) + stn * htmask_ref[...]
        yield

        ms_o = head_mean(o * o)
        y = y_intra + y_inter + dskip_ref[...] * xs
        y = y * _silu(P(Z0, SSM_WIDTH))
        ms_y = group_mean(y * y)
        cacc = conf_taps(cacc, 20, CONF_K)
        mu = jnp.mean(cacc, axis=-1, keepdims=True)
        xc = cacc - mu
        var = jnp.mean(xc * xc, axis=-1, keepdims=True)
        cval = _silu(xc * lax.rsqrt(var + LN_EPS) * lnw_ref[...] + lnb_ref[...])
        mix_scr[pl.ds(r0, C), GLA_WIDTH:GLA_WIDTH + CONF_CH] = cval.astype(mix_scr.dtype)
        o = o * lax.rsqrt(ms_o + EPS) * gnorm_ref[...] * _silu(P(GO0, GLA_WIDTH))
        mix_scr[pl.ds(r0, C), 0:GLA_WIDTH] = o.astype(mix_scr.dtype)
        y = y * lax.rsqrt(ms_y + EPS) * snorm_ref[...]
        mix_scr[pl.ds(r0, C), GLA_WIDTH + CONF_CH:D_MODEL] = y.astype(mix_scr.dtype)

    def load_conv_hist(slot, hconf, hsc):
        hist = jnp.concatenate([jnp.zeros((CONF_BASE - (CONF_K - 1), CONF_CH), F32), hconf], axis=0)
        ubuf[slot, 0, 0:CONF_BASE, :] = hist
        for s in range(1, SUBLANE):
            ubuf[slot, s, 0:CONF_BASE - SUBLANE, :] = hist[s:s + CONF_BASE - SUBLANE, :]
        xbuf[slot, 0:SSMC_BASE, :] = jnp.concatenate(
            [jnp.zeros((SSMC_BASE - (SSM_CONV - 1), SSM_XBC), F32), hsc], axis=0)

    def load_state(j, slot, first):
        if not first:
            load_conv_hist(slot, oconf_ref[j], osc_ref[j])
        elif cfg.zero_init:
            st_scr[slot] = jnp.zeros((GLA_WIDTH, QK_PAD), F32)
            ht_scr[slot] = jnp.zeros((SSM_GN, SSM_WIDTH), F32)
            load_conv_hist(slot, jnp.zeros((CONF_K - 1, CONF_CH), F32), jnp.zeros((SSM_CONV - 1, SSM_XBC), F32))
        else:
            rows = []
            for h in range(GLA_HEADS):
                parts = []
                if h > 0:
                    parts.append(jnp.zeros((GLA_DK, h * GLA_DV), F32))
                parts.append(sgla_ref[j, h])
                if h < GLA_HEADS - 1:
                    parts.append(jnp.zeros((GLA_DK, (GLA_HEADS - 1 - h) * GLA_DV), F32))
                rows.append(jnp.concatenate(parts, axis=1))
            rows.append(jnp.zeros((QK_PAD - GLA_QK, GLA_WIDTH), F32))
            st_scr[slot] = jnp.concatenate(rows, axis=0).T
            rows = []
            for h in range(SSM_HEADS):
                z = jnp.zeros((SSM_HEADDIM, SSM_STATE), F32)
                rows.append(jnp.concatenate([sssm_ref[j, h], z] if h // SSM_HPG == 0 else [z, sssm_ref[j, h]], axis=1))
            ht_scr[slot] = jnp.concatenate(rows, axis=0).T
            load_conv_hist(slot, sconf_ref[j], ssc_ref[j])

    def store_conv_state(j, slot):
        oconf_ref[j] = ubuf[slot, 0, T + CONF_BASE - (CONF_K - 1):T + CONF_BASE, :]
        osc_ref[j] = xbuf[slot, T + SSMC_BASE - (SSM_CONV - 1):T + SSMC_BASE, :]

    def store_state(j, slot):
        stt = st_scr[slot].T
        for h in range(GLA_HEADS):
            ogla_ref[j, h] = stt[h * GLA_DK:(h + 1) * GLA_DK, h * GLA_DV:(h + 1) * GLA_DV]
        htt = ht_scr[slot].T
        for h in range(SSM_HEADS):
            g = h // SSM_HPG
            ossm_ref[j, h] = htt[h * SSM_HEADDIM:(h + 1) * SSM_HEADDIM, g * SSM_STATE:(g + 1) * SSM_STATE]

    if bb == 1:
        if cfg.prefetch:
            @pl.when((pl.program_id(0) == 0) & (l == 0))
            def _():
                in_proj_full(0)
        else:
            in_proj_full(0)
        if nl == 1:
            load_state(0, 0, True)
        else:
            @pl.when(l == 0)
            def _():
                load_state(0, 0, True)

            @pl.when(l > 0)
            def _():
                load_state(0, 0, False)

        cpg = nchunk // ngroups
        work = collections.deque()
        if ngroups == 2:
            work.append(in_proj_pieces(1))

        def drain_work():
            while work:
                for _ in work.popleft():
                    pass

        def before_start(i):
            if ngroups == 2 and i == cpg:
                drain_work()

        def after_finish(i):
            if (i + 1) % cpg == 0:
                work.append(out_proj_pieces(i // cpg))
            if cfg.prefetch and i == cpg - 1:
                work.append(in_proj_pieces(0, xnext_ref))

        def work_pieces():
            while True:
                if work:
                    try:
                        next(work[0])
                    except StopIteration:
                        work.popleft()
                yield

        _run_staged([chunk_stages(ci * C, ci * C, 0) for ci in range(nchunk)], CHUNK_STAGE_LAG,
                    fillers=work_pieces(), before_start=before_start, after_finish=after_finish)
        drain_work()

        store_conv_state(0, 0)
        if nl == 1:
            store_state(0, 0)
        else:
            @pl.when(l == nl - 1)
            def _():
                store_state(0, 0)
    else:
        assert nl == 1 and nchunk == 1
        in_proj_full(0)

        def seq_stages(j):
            slot = j % SEQ_SLOTS
            load_state(j, slot, True)
            yield
            yield from chunk_stages(j * T, 0, slot)
            store_conv_state(j, slot)
            store_state(j, slot)

        _run_staged([seq_stages(j) for j in range(bb)], SEQ_STAGE_LAG)
        xo_ref[...] = x_ref[...] + _dot(mix_scr[...].astype(BF16), wout_ref[...])


def _np_consts(C):
    i = np.arange(C)
    tril = (i[None, :] <= i[:, None]).astype(np.float32)
    trilcat = np.tile(tril, (1, 6))
    eyecat = np.tile(np.eye(C, dtype=np.float32), (1, 6))
    hm6 = np.zeros((6 * C, QK_PAD), np.float32)
    gm6 = np.zeros((6 * C, SSM_GN), np.float32)
    for h in range(6):
        hm6[h * C:(h + 1) * C, h * GLA_DK:(h + 1) * GLA_DK] = 1.0
        g = h // SSM_HPG
        gm6[h * C:(h + 1) * C, g * SSM_STATE:(g + 1) * SSM_STATE] = 1.0
    stmask = np.zeros((GLA_WIDTH, QK_PAD), np.float32)
    htmask = np.zeros((SSM_GN, SSM_WIDTH), np.float32)
    for h in range(6):
        stmask[h * 64:(h + 1) * 64, h * 32:(h + 1) * 32] = 1.0
        g = h // SSM_HPG
        htmask[g * 64:(g + 1) * 64, h * 64:(h + 1) * 64] = 1.0
    return (jnp.asarray(tril, BF16), jnp.asarray(trilcat, F32), jnp.asarray(hm6, BF16), jnp.asarray(eyecat, F32),
            jnp.asarray(gm6, BF16), jnp.asarray(stmask, F32), jnp.asarray(htmask, F32))


def _const_spec(a):
    nd = a.ndim
    return pl.BlockSpec(a.shape, lambda i, l: (0,) * nd, pipeline_mode=pl.Buffered(1))


def _layer_spec(a, layer, block=None, col=0):
    shape = a.shape[1:] if block is None else block
    return pl.BlockSpec((None,) + tuple(shape), lambda i, l: (layer, 0, col), pipeline_mode=pl.Buffered(1))


def _state_spec(a, layer, bb):
    nrest = a.ndim - 2
    return pl.BlockSpec((None, bb) + tuple(a.shape[2:]), lambda i, l: (layer, i) + (0,) * nrest)


def _mixer_call(x2d, states_in, states_prev, wts, layer, Bn, L, bb, T, C):
    assert L % T == 0 and T % C == 0 and Bn % bb == 0 and C % SUBLANE == 0
    nl = L // T
    assert bb == 1 or nl == 1
    nb = Bn // bb
    R = bb * T
    nslot = 1 if bb == 1 else SEQ_SLOTS
    consts = _np_consts(C)
    x_spec = pl.BlockSpec((R, D_MODEL), lambda i, l: (i * nl + l, 0))
    state_shapes = ((DEPTH, Bn, GLA_HEADS, GLA_DK, GLA_DV), (DEPTH, Bn, CONF_K - 1, CONF_CH),
                    (DEPTH, Bn, SSM_CONV - 1, SSM_XBC), (DEPTH, Bn, SSM_HEADS, SSM_HEADDIM, SSM_STATE))
    out_shape = ([jax.ShapeDtypeStruct((Bn * L, D_MODEL), F32)]
                 + [jax.ShapeDtypeStruct(s, F32) for s in state_shapes])
    out_specs = [x_spec] + [_state_spec(s, layer, bb) for s in out_shape[1:]]

    ngroups = 2 if (bb == 1 and (T // C) % 2 == 0) else 1
    prefetch = ngroups == 2
    operands = [x2d]
    in_specs = [x_spec]
    if prefetch:
        rg = R // ngroups
        last_block = (Bn * L) // rg - ngroups
        operands.append(x2d)
        in_specs.append(pl.BlockSpec(
            (rg, D_MODEL), lambda i, l: (jnp.minimum((i * nl + l + 1) * ngroups, last_block), 0)))
    if states_in is not None:
        operands += list(states_in)
        in_specs += [_state_spec(a, layer, bb) for a in states_in]
    operands += list(wts)
    in_specs += [_layer_spec(a, layer) for a in wts]
    operands += list(consts)
    in_specs += [_const_spec(a) for a in consts]
    aliases = {}
    if states_prev is not None:
        for k, a in enumerate(states_prev):
            aliases[len(operands)] = 1 + k
            operands.append(a)
            in_specs.append(pl.BlockSpec(memory_space=pl.ANY))
    cfg = MixerCfg(C, T, bb, nl, states_in is None, len(aliases), prefetch)
    scratch = [
        pltpu.VMEM((R, NP), F32),
        pltpu.VMEM((R // ngroups, D_MODEL), BF16),
        pltpu.VMEM((nslot, GLA_WIDTH, QK_PAD), F32),
        pltpu.VMEM((nslot, SSM_GN, SSM_WIDTH), F32),
        pltpu.VMEM((nslot, SUBLANE, CONF_BASE + T, CONF_CH), F32),
        pltpu.VMEM((nslot, SSMC_BASE + T, SSM_XBC), F32),
        pltpu.VMEM((R, D_MODEL), BF16 if C % (2 * SUBLANE) == 0 else F32),
    ]
    return pl.pallas_call(
        functools.partial(_mixer_kernel, cfg),
        grid=(nb, nl),
        in_specs=in_specs,
        out_specs=out_specs,
        out_shape=out_shape,
        scratch_shapes=scratch,
        input_output_aliases=aliases,
        compiler_params=pltpu.CompilerParams(
            dimension_semantics=("arbitrary", "arbitrary"), vmem_limit_bytes=VMEM_LIMIT_BYTES),
        name="mixer",
    )(*operands)


FfnCfg = collections.namedtuple("FfnCfg", "T bb nl final zero_init n_alias")


def _ffn_kernel(cfg, *refs):
    T, bb, nl = cfg.T, cfg.bb, cfg.nl
    refs = list(refs)
    x_ref = refs.pop(0)
    sffn_ref = None if cfg.zero_init else refs.pop(0)
    n2_ref, wa_ref, wg_ref, fcw_ref, fcb_ref, wo_ref, fn_ref = refs[:7]
    del refs[:7]
    del refs[:cfg.n_alias]
    xo_ref, offn_ref, abuf, act = refs

    l = pl.program_id(1)
    R = bb * T
    x = x_ref[...]
    xn = (_rms_scale(x) * n2_ref[...]).astype(BF16)
    lo = FFNC_BASE - (FFN_CONV - 1)

    def init_hist():
        if cfg.zero_init:
            abuf[:, lo:FFNC_BASE, :] = jnp.zeros((bb, FFN_CONV - 1, D_FF), F32)
        else:
            abuf[:, lo:FFNC_BASE, :] = sffn_ref[...]

    if nl == 1:
        init_hist()
    else:
        @pl.when(l == 0)
        def _():
            init_hist()

        @pl.when(l > 0)
        def _():
            abuf[:, lo:FFNC_BASE, :] = offn_ref[...]
    f0 = 0
    for fw in FFN_SPLITS:
        a = _dot(xn, wa_ref[:, f0:f0 + fw])
        g = _dot(xn, wg_ref[:, f0:f0 + fw])
        abuf[:, FFNC_BASE:FFNC_BASE + T, f0:f0 + fw] = a.reshape(bb, T, fw)
        ac = fcb_ref[:, f0:f0 + fw][None]
        for kk in range(FFN_CONV):
            ac = ac + fcw_ref[kk:kk + 1, f0:f0 + fw][None] * abuf[:, lo + kk:lo + kk + T, f0:f0 + fw]
        offn_ref[:, :, f0:f0 + fw] = abuf[:, T + lo:T + FFNC_BASE, f0:f0 + fw]
        act[:, f0:f0 + fw] = (_silu(ac).reshape(R, fw) * g).astype(BF16)
        f0 += fw
    y = x + _dot(act[...], wo_ref[...])
    if cfg.final:
        y = _rms_scale(y) * fn_ref[...]
    xo_ref[...] = y


def _ffn_call(x2d, state_in, state_prev, wts, layer, Bn, L, bb, T, final):
    assert L % T == 0 and Bn % bb == 0
    nl = L // T
    assert bb == 1 or nl == 1
    nb = Bn // bb
    R = bb * T
    n2, w_ffn_in, fcw, fcb, w_ffn_out, fn = wts
    x_spec = pl.BlockSpec((R, D_MODEL), lambda i, l: (i * nl + l, 0))
    st_shape = jax.ShapeDtypeStruct((DEPTH, Bn, FFN_CONV - 1, D_FF), F32)
    operands = [x2d]
    in_specs = [x_spec]
    if state_in is not None:
        operands.append(state_in)
        in_specs.append(_state_spec(state_in, layer, bb))
    operands += [n2, w_ffn_in, w_ffn_in, fcw, fcb, w_ffn_out, fn]
    in_specs += [_layer_spec(n2, layer),
                 _layer_spec(w_ffn_in, layer, block=(D_MODEL, D_FF), col=0),
                 _layer_spec(w_ffn_in, layer, block=(D_MODEL, D_FF), col=1),
                 _layer_spec(fcw, layer), _layer_spec(fcb, layer), _layer_spec(w_ffn_out, layer), _const_spec(fn)]
    aliases = {}
    if state_prev is not None:
        aliases[len(operands)] = 1
        operands.append(state_prev)
        in_specs.append(pl.BlockSpec(memory_space=pl.ANY))
    cfg = FfnCfg(T, bb, nl, final, state_in is None, len(aliases))
    return pl.pallas_call(
        functools.partial(_ffn_kernel, cfg),
        grid=(nb, nl),
        in_specs=in_specs,
        out_specs=[x_spec, _state_spec(st_shape, layer, bb)],
        out_shape=[jax.ShapeDtypeStruct((Bn * L, D_MODEL), F32), st_shape],
        scratch_shapes=[pltpu.VMEM((bb, FFNC_BASE + T, D_FF), F32),
                        pltpu.VMEM((R, D_FF), BF16)],
        input_output_aliases=aliases,
        compiler_params=pltpu.CompilerParams(
            dimension_semantics=("arbitrary", "arbitrary"), vmem_limit_bytes=VMEM_LIMIT_BYTES),
        name="ffn",
    )(*operands)


def _prep_weights(norm1, w_in, w_gk, b_gk, gla_norm, conf_w, conf_b, conf_ln_w, conf_ln_b,
                  ssm_conv_w, ssm_conv_b, dt_bias, a_log, d_skip, ssm_norm, w_out,
                  norm2, w_ffn_in, ffn_conv_w, ffn_conv_b, w_ffn_out, final_norm):
    offs = np.concatenate([[0], np.cumsum(IN_SIZES)])
    wb = w_in.astype(BF16)
    seg = lambda i: wb[:, :, int(offs[i]):int(offs[i + 1])]
    zpad = lambda n: jnp.zeros((DEPTH, D_MODEL, n), BF16)
    w_in_p = jnp.concatenate([
        seg(0), zpad(QK_PAD - GLA_QK), seg(1), zpad(QK_PAD - GLA_QK), seg(2), seg(3),
        seg(5), seg(6), seg(7), seg(8), jnp.repeat(seg(9), SSM_HEADDIM, axis=2),
        seg(4), zpad(LANE - GLA_RANK)], axis=2)
    assert w_in_p.shape == (DEPTH, D_MODEL, NP)
    row = lambda v: v.reshape(DEPTH, 1, -1).astype(F32)
    wgk_p = jnp.zeros((DEPTH, LANE, QK_PAD), F32).at[:, :GLA_RANK, :GLA_QK].set(w_gk).astype(BF16)
    bgk_p = jnp.zeros((DEPTH, 1, QK_PAD), F32).at[:, 0, :GLA_QK].set(b_gk)
    rep64 = lambda v: row(jnp.repeat(v, SSM_HEADDIM, axis=1))
    mixer_w = (row(norm1), w_in_p, wgk_p, bgk_p, row(jnp.tile(gla_norm, (1, GLA_HEADS))),
               conf_w.astype(F32), row(conf_b), row(conf_ln_w), row(conf_ln_b),
               ssm_conv_w.astype(F32), row(ssm_conv_b), rep64(dt_bias), rep64(a_log),
               rep64(d_skip), row(ssm_norm), w_out.astype(BF16))
    ffn_w = (row(norm2), w_ffn_in.astype(BF16), ffn_conv_w.astype(F32), row(ffn_conv_b),
             w_ffn_out.astype(BF16), final_norm.reshape(1, -1).astype(F32))
    return mixer_w, ffn_w


def _run_group(x, states, mixer_w, ffn_w, bb, T, C, ffn_bb=None, ffn_T=None):
    ffn_bb = bb if ffn_bb is None else ffn_bb
    ffn_T = T if ffn_T is None else ffn_T
    Bn, L, _ = x.shape
    x2d = x.reshape(Bn * L, D_MODEL)
    out_states = None
    for layer in range(DEPTH):
        prev4 = None if out_states is None else out_states[:4]
        prev_ffn = None if out_states is None else out_states[4]
        x2d, ngla, nconf, nsc, nssm = _mixer_call(
            x2d, None if states is None else states[:4], prev4, mixer_w, layer, Bn, L, bb, T, C)
        x2d, nffn = _ffn_call(x2d, None if states is None else states[4], prev_ffn, ffn_w, layer, Bn, L,
                              ffn_bb, ffn_T, final=(layer == DEPTH - 1))
        out_states = (ngla, nconf, nsc, nssm, nffn)
    return x2d.reshape(Bn, L, D_MODEL), out_states


FFN_ROWS = 512


def _tiles(Lp, Bs, Ls):
    tp = min(512, Lp)
    return tp, math.gcd(tp, GLA_CHUNK), min(16, Bs), math.gcd(Ls, GLA_CHUNK), min(max(FFN_ROWS // Ls, 1), Bs)


def kernel(x_prompt, x_sample, state_gla, state_conf_conv, state_ssm_conv, state_ssm, state_ffn_conv, norm1, w_in, w_gk, b_gk, gla_norm, conf_w, conf_b, conf_ln_w, conf_ln_b, ssm_conv_w, ssm_conv_b, dt_bias, a_log, d_skip, ssm_norm, w_out, norm2, w_ffn_in, ffn_conv_w, ffn_conv_b, w_ffn_out, final_norm):
    mixer_w, ffn_w = _prep_weights(norm1, w_in, w_gk, b_gk, gla_norm, conf_w, conf_b, conf_ln_w, conf_ln_b,
                                   ssm_conv_w, ssm_conv_b, dt_bias, a_log, d_skip, ssm_norm, w_out,
                                   norm2, w_ffn_in, ffn_conv_w, ffn_conv_b, w_ffn_out, final_norm)
    tp, cp, bbs, cs, bbf = _tiles(x_prompt.shape[1], x_sample.shape[0], x_sample.shape[1])
    y_p, st_p = _run_group(x_prompt, None, mixer_w, ffn_w, 1, tp, cp, ffn_T=min(2 * FFN_ROWS, x_prompt.shape[1]))
    s_states = (state_gla, state_conf_conv, state_ssm_conv, state_ssm, state_ffn_conv)
    y_s, st_s = _run_group(x_sample, s_states, mixer_w, ffn_w, bbs, x_sample.shape[1], cs, ffn_bb=bbf)
    return (y_p, y_s) + tuple(st_p) + tuple(st_s)
```

```python
import collections
import functools
import math

import numpy as np
import jax
import jax.numpy as jnp
from jax import lax
from jax.experimental import pallas as pl
from jax.experimental.pallas import tpu as pltpu

F32 = jnp.float32
BF16 = jnp.bfloat16

D_MODEL = 1024
DEPTH = 2
GLA_HEADS = 6
GLA_DK = 32
GLA_DV = 64
GLA_QK = GLA_HEADS * GLA_DK
GLA_WIDTH = GLA_HEADS * GLA_DV
GLA_RANK = 16
GLA_GATE_NORM = 16.0
GLA_CHUNK = 64
CONF_CH = 256
CONF_K = 31
LN_EPS = 1e-5
SSM_WIDTH = 384
SSM_HEADDIM = 64
SSM_HEADS = 6
SSM_GROUPS = 2
SSM_HPG = 3
SSM_STATE = 64
SSM_GN = SSM_GROUPS * SSM_STATE
SSM_XBC = SSM_WIDTH + 2 * SSM_GN
SSM_CONV = 4
SSM_CHUNK = 64
D_FF = 2816
FFN_CONV = 3
EPS = 1e-6
IN_SIZES = (GLA_QK, GLA_QK, GLA_WIDTH, GLA_WIDTH, GLA_RANK, CONF_CH, CONF_CH, SSM_WIDTH, SSM_XBC, SSM_HEADS)

LANE = 128
SUBLANE = 8
MXU_DIM = 256
VMEM_LIMIT_BYTES = 60 * 1024 * 1024

QK_PAD = 256
Q0 = 0
K0 = Q0 + QK_PAD
V0 = K0 + QK_PAD
GO0 = V0 + GLA_WIDTH
CA0 = GO0 + GLA_WIDTH
CB0 = CA0 + CONF_CH
Z0 = CB0 + CONF_CH
XBC0 = Z0 + SSM_WIDTH
DT0 = XBC0 + SSM_XBC
GLR0 = DT0 + SSM_WIDTH
NP = GLR0 + LANE

SPLIT_PARTS = 2

FFN_SPLITS = (768, 768, 768, 512)

CONF_BASE = 32
SSMC_BASE = 8
FFNC_BASE = 8

CHUNK_STAGE_LAG = 3
SEQ_STAGES = 6
SEQ_STAGE_LAG = 1
SEQ_SLOTS = -(-SEQ_STAGES // SEQ_STAGE_LAG)
PROJ_PIECES_PER_TICK = 2


def _dot(a, b):
    return jnp.dot(a, b, preferred_element_type=F32)


def _dot_nt(a, b):
    return lax.dot_general(a, b, (((1,), (1,)), ((), ())), preferred_element_type=F32)


def _dot_tn(a, b):
    return lax.dot_general(a, b, (((0,), (0,)), ((), ())), preferred_element_type=F32)


def _split_bf16(a, parts):
    out = []
    r = a
    for i in range(parts):
        p = r.astype(BF16)
        out.append(p)
        if i + 1 < parts:
            r = r - p.astype(F32)
    return out


def _sigmoid(x):
    return 1.0 / (1.0 + jnp.exp(-x))


def _silu(x):
    return x * _sigmoid(x)


def _softplus(x):
    return jnp.maximum(x, 0.0) + jnp.log(1.0 + jnp.exp(-jnp.abs(x)))


def _log_sigmoid(x):
    return jnp.minimum(x, 0.0) - jnp.log(1.0 + jnp.exp(-jnp.abs(x)))


def _rms_scale(x):
    return x * lax.rsqrt(jnp.mean(x * x, axis=-1, keepdims=True) + EPS)


def _per_head_matmul(cat, rhs, c):
    first_head = lax.broadcasted_iota(jnp.int32, (c, LANE), 1) < 64
    packed = c % (2 * SUBLANE) == 0
    if packed:
        keep_lo = jnp.where(first_head, 1.0, 0.0).astype(BF16)
        keep_hi = jnp.where(first_head, 0.0, 1.0).astype(BF16)
    cols = []
    for col in range(3):
        r = rhs[:, col * LANE:(col + 1) * LANE]
        if packed:
            rb = r.astype(BF16)
            two = jnp.concatenate([rb * keep_lo, rb * keep_hi], axis=0)
        else:
            two = jnp.concatenate([jnp.where(first_head, r, 0.0), jnp.where(first_head, 0.0, r)],
                                  axis=0).astype(BF16)
        cols.append(_dot(cat[:, 2 * col * c:(2 * col + 2) * c], two))
    return jnp.concatenate(cols, axis=1)


def _run_staged(chunk_gens, lag, fillers=None, before_start=None, after_finish=None):
    pending = list(enumerate(chunk_gens))
    active = []
    tick = 0
    while pending or active:
        if pending and tick % lag == 0:
            i, g = pending.pop(0)
            if before_start is not None:
                before_start(i)
            active.append((i, g))
        for item in list(active):
            i, g = item
            try:
                next(g)
            except StopIteration:
                active.remove(item)
                if after_finish is not None:
                    after_finish(i)
        if fillers is not None:
            next(fillers, None)
        tick += 1


MixerCfg = collections.namedtuple("MixerCfg", "C T bb nl zero_init n_alias prefetch")


def _mixer_kernel(cfg, *refs):
    C, T, bb, nl = cfg.C, cfg.T, cfg.bb, cfg.nl
    refs = list(refs)
    x_ref = refs.pop(0)
    xnext_ref = refs.pop(0) if cfg.prefetch else None
    if cfg.zero_init:
        sgla_ref = sconf_ref = ssc_ref = sssm_ref = None
    else:
        sgla_ref, sconf_ref, ssc_ref, sssm_ref = refs[:4]
        del refs[:4]
    (n1_ref, win_ref, wgk_ref, bgk_ref, gnorm_ref, cw_ref, cbias_ref, lnw_ref, lnb_ref,
     scw_ref, scb_ref, dtb_ref, alog_ref, dskip_ref, snorm_ref, wout_ref) = refs[:16]
    del refs[:16]
    (tril_ref, trilcat_ref, hm6_ref, eyecat_ref, gm6_ref, stmask_ref, htmask_ref) = refs[:7]
    del refs[:7]
    del refs[:cfg.n_alias]
    xo_ref, ogla_ref, oconf_ref, osc_ref, ossm_ref = refs[:5]
    proj_scr, xn_scr, st_scr, ht_scr, ubuf, xbuf, mix_scr = refs[5:]

    l = pl.program_id(1)
    nchunk = T // C
    ngroups = 2 if (bb == 1 and nchunk % 2 == 0) else 1
    RG = bb * T // ngroups
    NPIECE = NP // MXU_DIM

    def in_proj_full(g):
        rows = slice(g * RG, (g + 1) * RG)
        xn = _rms_scale(x_ref[rows, :]) * n1_ref[...]
        proj_scr[rows, :] = _dot(xn.astype(BF16), win_ref[...])

    def in_proj_pieces(g, src_ref=None):
        rows = slice(g * RG, (g + 1) * RG)
        src = x_ref[rows, :] if src_ref is None else src_ref[...]
        xn_scr[...] = (_rms_scale(src) * n1_ref[...]).astype(BF16)
        for p in range(NPIECE):
            cols = slice(p * MXU_DIM, (p + 1) * MXU_DIM)
            proj_scr[rows, cols] = _dot(xn_scr[...], win_ref[:, cols])
            if (p + 1) % PROJ_PIECES_PER_TICK == 0:
                yield

    def out_proj_pieces(g):
        rows = slice(g * RG, (g + 1) * RG)
        for p in range(D_MODEL // MXU_DIM):
            cols = slice(p * MXU_DIM, (p + 1) * MXU_DIM)
            xo_ref[rows, cols] = x_ref[rows, cols] + _dot(mix_scr[rows, :].astype(BF16), wout_ref[:, cols])
            yield

    tril = tril_ref[...]
    trilcat = trilcat_ref[...] > 0.5

    def cumsum_rows(a):
        acc = None
        for p in _split_bf16(a, SPLIT_PARTS):
            t = _dot(tril, p)
            acc = t if acc is None else acc + t
        return acc

    first_half = lax.broadcasted_iota(jnp.int32, (C, LANE), 1) < 64

    def stack_masked(a, mask_ref, n):
        if C % (2 * SUBLANE) == 0:
            return jnp.concatenate([a.astype(BF16)] * n, axis=0) * mask_ref[...]
        return (jnp.concatenate([a] * n, axis=0) * mask_ref[...].astype(F32)).astype(BF16)

    def head_mean(a):
        cols = []
        for col in range(3):
            blk = a[:, col * LANE:(col + 1) * LANE]
            lo = jnp.sum(jnp.where(first_half, blk, 0.0), axis=-1, keepdims=True)
            hi = jnp.sum(jnp.where(first_half, 0.0, blk), axis=-1, keepdims=True)
            cols.append(jnp.where(first_half, lo, hi))
        return jnp.concatenate(cols, axis=1) * (1.0 / 64)

    def group_mean(a):
        c0, c1, c2 = (a[:, col * LANE:(col + 1) * LANE] for col in range(3))
        g0 = jnp.sum(c0 + jnp.where(first_half, c1, 0.0), axis=-1, keepdims=True)
        g1 = jnp.sum(c2 + jnp.where(first_half, 0.0, c1), axis=-1, keepdims=True)
        shape = (C, LANE)
        return jnp.concatenate([jnp.broadcast_to(g0, shape), jnp.where(first_half, g0, g1),
                                jnp.broadcast_to(g1, shape)], axis=1) * (1.0 / (SSM_HPG * SSM_HEADDIM))

    def chunk_stages(r0, c0, slot):
        def P(off, width):
            return proj_scr[pl.ds(r0, C), off:off + width]

        gk_pre = _dot(P(GLR0, LANE).astype(BF16), wgk_ref[...]) + bgk_ref[...]
        xbc = P(XBC0, SSM_XBC)
        ext = jnp.concatenate([xbuf[slot, pl.ds(c0, SUBLANE), :], xbc], axis=0)
        xbuf[slot, pl.ds(SSMC_BASE + c0, C), :] = xbc
        acc = jnp.broadcast_to(scb_ref[...], (C, SSM_XBC))
        for kk in range(SSM_CONV):
            lo = SUBLANE - (SSM_CONV - 1) + kk
            acc = acc + scw_ref[kk:kk + 1, :] * ext[lo:lo + C, :]
        xcv = _silu(acc)
        xs = xcv[:, 0:SSM_WIDTH]
        bm = xcv[:, SSM_WIDTH:SSM_WIDTH + SSM_GN]
        cm = xcv[:, SSM_WIDTH + SSM_GN:SSM_XBC]
        dtf = _softplus(P(DT0, SSM_WIDTH) + dtb_ref[...])
        af = dtf * (-jnp.exp(alog_ref[...]))
        yield

        gk = _log_sigmoid(gk_pre) * (1.0 / GLA_GATE_NORM)
        acum = cumsum_rows(af)
        b = cumsum_rows(gk)
        u = P(CA0, CONF_CH) * _sigmoid(P(CB0, CONF_CH))
        b0 = CONF_BASE + c0
        uext = jnp.concatenate([ubuf[slot, 0, pl.ds(b0 - SUBLANE, SUBLANE), :], u], axis=0)
        ubuf[slot, 0, pl.ds(b0, C), :] = u
        for s in range(1, SUBLANE):
            ubuf[slot, s, pl.ds(b0 - SUBLANE, C), :] = uext[s:s + C, :]
        xdt = xs * dtf
        bbd = stack_masked(bm, gm6_ref, SSM_HEADS)
        cb_cat = _dot_nt(cm.astype(BF16), bbd)

        def conf_taps(acc, k_lo, k_hi):
            for kk in range(k_lo, k_hi):
                s = (kk - (CONF_K - 1)) % SUBLANE
                off = kk - (CONF_K - 1) - s
                acc = acc + cw_ref[kk:kk + 1, :] * ubuf[slot, s, pl.ds(b0 + off, C), :]
            return acc
        yield

        alast = acum[C - 1:C, :]
        if C == SSM_CHUNK:
            col_cat = acum
            acum_t = acum.T
            row_cat = jnp.concatenate([acum_t[h * 64:h * 64 + 1, :] for h in range(SSM_HEADS)], axis=1)
        else:
            col_cat = jnp.concatenate([acum[:, h * 64:h * 64 + C] for h in range(SSM_HEADS)], axis=1)
            ones_cc = jnp.ones((C, C), BF16)
            row_cat = None
            for p in _split_bf16(col_cat * eyecat_ref[...], SPLIT_PARTS):
                t = _dot(ones_cc, p)
                row_cat = t if row_cat is None else row_cat + t
        lmat = jnp.exp(jnp.where(trilcat, col_cat - row_cat, -jnp.inf))
        blast = b[C - 1:C, :]
        q_in = P(Q0, QK_PAD) * (GLA_DK ** -0.5) * jnp.exp(b)
        k = P(K0, QK_PAD)
        k_in = k * jnp.exp(-b)
        k_out = k * jnp.exp(blast - b)
        v = P(V0, GLA_WIDTH)
        vb = v.astype(BF16)
        q_inb = q_in.astype(BF16)
        kbd = stack_masked(k_in, hm6_ref, GLA_HEADS)
        att = _dot_nt(q_inb, kbd)
        cacc = conf_taps(jnp.broadcast_to(cbias_ref[...], (C, CONF_CH)), 0, 10)
        yield

        att = jnp.where(trilcat, att, 0.0)
        o_intra = _per_head_matmul(att.astype(BF16), v, C)
        y_intra = _per_head_matmul((cb_cat * lmat).astype(BF16), xdt, C)
        k_outb = k_out.astype(BF16)
        decay = jnp.exp(blast)
        o_inter = []
        for r_lo, r_hi, l_lo in ((0, 4 * GLA_DV, 0), (4 * GLA_DV, GLA_WIDTH, LANE)):
            st = st_scr[slot, r_lo:r_hi, l_lo:l_lo + LANE]
            o_inter.append(_dot_nt(q_inb[:, l_lo:l_lo + LANE], st.astype(BF16)))
            kv = _dot_tn(vb[:, r_lo:r_hi], k_outb[:, l_lo:l_lo + LANE])
            st_scr[slot, r_lo:r_hi, l_lo:l_lo + LANE] = (
                st * decay[:, l_lo:l_lo + LANE] + kv * stmask_ref[r_lo:r_hi, l_lo:l_lo + LANE])
        o = o_intra + jnp.concatenate(o_inter, axis=1)
        ht = ht_scr[slot]
        y_inter = _dot(cm.astype(BF16), ht.astype(BF16)) * jnp.exp(acum)
        cacc = conf_taps(cacc, 10, 20)
        wx = xdt * jnp.exp(alast - acum)
        stn = _dot_tn(bm.astype(BF16), wx.astype(BF16))
        ht_scr[slot] = ht * jnp.exp(alast) + stn * htmask_ref[...]
        yield

        ms_o = head_mean(o * o)
        y = y_intra + y_inter + dskip_ref[...] * xs
        y = y * _silu(P(Z0, SSM_WIDTH))
        ms_y = group_mean(y * y)
        cacc = conf_taps(cacc, 20, CONF_K)
        mu = jnp.mean(cacc, axis=-1, keepdims=True)
        xc = cacc - mu
        var = jnp.mean(xc * xc, axis=-1, keepdims=True)
        cval = _silu(xc * lax.rsqrt(var + LN_EPS) * lnw_ref[...] + lnb_ref[...])
        mix_scr[pl.ds(r0, C), GLA_WIDTH:GLA_WIDTH + CONF_CH] = cval.astype(mix_scr.dtype)
        o = o * lax.rsqrt(ms_o + EPS) * gnorm_ref[...] * _silu(P(GO0, GLA_WIDTH))
        mix_scr[pl.ds(r0, C), 0:GLA_WIDTH] = o.astype(mix_scr.dtype)
        y = y * lax.rsqrt(ms_y + EPS) * snorm_ref[...]
        mix_scr[pl.ds(r0, C), GLA_WIDTH + CONF_CH:D_MODEL] = y.astype(mix_scr.dtype)

    def load_conv_hist(slot, hconf, hsc):
        hist = jnp.concatenate([jnp.zeros((CONF_BASE - (CONF_K - 1), CONF_CH), F32), hconf], axis=0)
        ubuf[slot, 0, 0:CONF_BASE, :] = hist
        for s in range(1, SUBLANE):
            ubuf[slot, s, 0:CONF_BASE - SUBLANE, :] = hist[s:s + CONF_BASE - SUBLANE, :]
        xbuf[slot, 0:SSMC_BASE, :] = jnp.concatenate(
            [jnp.zeros((SSMC_BASE - (SSM_CONV - 1), SSM_XBC), F32), hsc], axis=0)

    def load_state(j, slot, first):
        if not first:
            load_conv_hist(slot, oconf_ref[j], osc_ref[j])
        elif cfg.zero_init:
            st_scr[slot] = jnp.zeros((GLA_WIDTH, QK_PAD), F32)
            ht_scr[slot] = jnp.zeros((SSM_GN, SSM_WIDTH), F32)
            load_conv_hist(slot, jnp.zeros((CONF_K - 1, CONF_CH), F32), jnp.zeros((SSM_CONV - 1, SSM_XBC), F32))
        else:
            rows = []
            for h in range(GLA_HEADS):
                parts = []
                if h > 0:
                    parts.append(jnp.zeros((GLA_DK, h * GLA_DV), F32))
                parts.append(sgla_ref[j, h])
                if h < GLA_HEADS - 1:
                    parts.append(jnp.zeros((GLA_DK, (GLA_HEADS - 1 - h) * GLA_DV), F32))
                rows.append(jnp.concatenate(parts, axis=1))
            rows.append(jnp.zeros((QK_PAD - GLA_QK, GLA_WIDTH), F32))
            st_scr[slot] = jnp.concatenate(rows, axis=0).T
            rows = []
            for h in range(SSM_HEADS):
                z = jnp.zeros((SSM_HEADDIM, SSM_STATE), F32)
                rows.append(jnp.concatenate([sssm_ref[j, h], z] if h // SSM_HPG == 0 else [z, sssm_ref[j, h]], axis=1))
            ht_scr[slot] = jnp.concatenate(rows, axis=0).T
            load_conv_hist(slot, sconf_ref[j], ssc_ref[j])

    def store_conv_state(j, slot):
        oconf_ref[j] = ubuf[slot, 0, T + CONF_BASE - (CONF_K - 1):T + CONF_BASE, :]
        osc_ref[j] = xbuf[slot, T + SSMC_BASE - (SSM_CONV - 1):T + SSMC_BASE, :]

    def store_state(j, slot):
        stt = st_scr[slot].T
        for h in range(GLA_HEADS):
            ogla_ref[j, h] = stt[h * GLA_DK:(h + 1) * GLA_DK, h * GLA_DV:(h + 1) * GLA_DV]
        htt = ht_scr[slot].T
        for h in range(SSM_HEADS):
            g = h // SSM_HPG
            ossm_ref[j, h] = htt[h * SSM_HEADDIM:(h + 1) * SSM_HEADDIM, g * SSM_STATE:(g + 1) * SSM_STATE]

    if bb == 1:
        if cfg.prefetch:
            @pl.when((pl.program_id(0) == 0) & (l == 0))
            def _():
                in_proj_full(0)
        else:
            in_proj_full(0)
        if nl == 1:
            load_state(0, 0, True)
        else:
            @pl.when(l == 0)
            def _():
                load_state(0, 0, True)

            @pl.when(l > 0)
            def _():
                load_state(0, 0, False)

        cpg = nchunk // ngroups
        work = collections.deque()
        if ngroups == 2:
            work.append(in_proj_pieces(1))

        def drain_work():
            while work:
                for _ in work.popleft():
                    pass

        def before_start(i):
            if ngroups == 2 and i == cpg:
                drain_work()

        def after_finish(i):
            if (i + 1) % cpg == 0:
                work.append(out_proj_pieces(i // cpg))
            if cfg.prefetch and i == cpg - 1:
                work.append(in_proj_pieces(0, xnext_ref))

        def work_pieces():
            while True:
                if work:
                    try:
                        next(work[0])
                    except StopIteration:
                        work.popleft()
                yield

        _run_staged([chunk_stages(ci * C, ci * C, 0) for ci in range(nchunk)], CHUNK_STAGE_LAG,
                    fillers=work_pieces(), before_start=before_start, after_finish=after_finish)
        drain_work()

        store_conv_state(0, 0)
        if nl == 1:
            store_state(0, 0)
        else:
            @pl.when(l == nl - 1)
            def _():
                store_state(0, 0)
    else:
        assert nl == 1 and nchunk == 1
        in_proj_full(0)

        def seq_stages(j):
            slot = j % SEQ_SLOTS
            load_state(j, slot, True)
            yield
            yield from chunk_stages(j * T, 0, slot)
            store_conv_state(j, slot)
            store_state(j, slot)

        _run_staged([seq_stages(j) for j in range(bb)], SEQ_STAGE_LAG)
        xo_ref[...] = x_ref[...] + _dot(mix_scr[...].astype(BF16), wout_ref[...])


def _np_consts(C):
    i = np.arange(C)
    tril = (i[None, :] <= i[:, None]).astype(np.float32)
    trilcat = np.tile(tril, (1, 6))
    eyecat = np.tile(np.eye(C, dtype=np.float32), (1, 6))
    hm6 = np.zeros((6 * C, QK_PAD), np.float32)
    gm6 = np.zeros((6 * C, SSM_GN), np.float32)
    for h in range(6):
        hm6[h * C:(h + 1) * C, h * GLA_DK:(h + 1) * GLA_DK] = 1.0
        g = h // SSM_HPG
        gm6[h * C:(h + 1) * C, g * SSM_STATE:(g + 1) * SSM_STATE] = 1.0
    stmask = np.zeros((GLA_WIDTH, QK_PAD), np.float32)
    htmask = np.zeros((SSM_GN, SSM_WIDTH), np.float32)
    for h in range(6):
        stmask[h * 64:(h + 1) * 64, h * 32:(h + 1) * 32] = 1.0
        g = h // SSM_HPG
        htmask[g * 64:(g + 1) * 64, h * 64:(h + 1) * 64] = 1.0
    return (jnp.asarray(tril, BF16), jnp.asarray(trilcat, F32), jnp.asarray(hm6, BF16), jnp.asarray(eyecat, F32),
            jnp.asarray(gm6, BF16), jnp.asarray(stmask, F32), jnp.asarray(htmask, F32))


def _const_spec(a):
    nd = a.ndim
    return pl.BlockSpec(a.shape, lambda i, l: (0,) * nd, pipeline_mode=pl.Buffered(1))


def _layer_spec(a, layer, block=None, col=0):
    shape = a.shape[1:] if block is None else block
    return pl.BlockSpec((None,) + tuple(shape), lambda i, l: (layer, 0, col), pipeline_mode=pl.Buffered(1))


def _state_spec(a, layer, bb):
    nrest = a.ndim - 2
    return pl.BlockSpec((None, bb) + tuple(a.shape[2:]), lambda i, l: (layer, i) + (0,) * nrest)


def _mixer_call(x2d, states_in, states_prev, wts, layer, Bn, L, bb, T, C):
    assert L % T == 0 and T % C == 0 and Bn % bb == 0 and C % SUBLANE == 0
    nl = L // T
    assert bb == 1 or nl == 1
    nb = Bn // bb
    R = bb * T
    nslot = 1 if bb == 1 else SEQ_SLOTS
    consts = _np_consts(C)
    x_spec = pl.BlockSpec((R, D_MODEL), lambda i, l: (i * nl + l, 0))
    state_shapes = ((DEPTH, Bn, GLA_HEADS, GLA_DK, GLA_DV), (DEPTH, Bn, CONF_K - 1, CONF_CH),
                    (DEPTH, Bn, SSM_CONV - 1, SSM_XBC), (DEPTH, Bn, SSM_HEADS, SSM_HEADDIM, SSM_STATE))
    out_shape = ([jax.ShapeDtypeStruct((Bn * L, D_MODEL), F32)]
                 + [jax.ShapeDtypeStruct(s, F32) for s in state_shapes])
    out_specs = [x_spec] + [_state_spec(s, layer, bb) for s in out_shape[1:]]

    ngroups = 2 if (bb == 1 and (T // C) % 2 == 0) else 1
    prefetch = ngroups == 2
    operands = [x2d]
    in_specs = [x_spec]
    if prefetch:
        rg = R // ngroups
        last_block = (Bn * L) // rg - ngroups
        operands.append(x2d)
        in_specs.append(pl.BlockSpec(
            (rg, D_MODEL), lambda i, l: (jnp.minimum((i * nl + l + 1) * ngroups, last_block), 0)))
    if states_in is not None:
        operands += list(states_in)
        in_specs += [_state_spec(a, layer, bb) for a in states_in]
    operands += list(wts)
    in_specs += [_layer_spec(a, layer) for a in wts]
    operands += list(consts)
    in_specs += [_const_spec(a) for a in consts]
    aliases = {}
    if states_prev is not None:
        for k, a in enumerate(states_prev):
            aliases[len(operands)] = 1 + k
            operands.append(a)
            in_specs.append(pl.BlockSpec(memory_space=pl.ANY))
    cfg = MixerCfg(C, T, bb, nl, states_in is None, len(aliases), prefetch)
    scratch = [
        pltpu.VMEM((R, NP), F32),
        pltpu.VMEM((R // ngroups, D_MODEL), BF16),
        pltpu.VMEM((nslot, GLA_WIDTH, QK_PAD), F32),
        pltpu.VMEM((nslot, SSM_GN, SSM_WIDTH), F32),
        pltpu.VMEM((nslot, SUBLANE, CONF_BASE + T, CONF_CH), F32),
        pltpu.VMEM((nslot, SSMC_BASE + T, SSM_XBC), F32),
        pltpu.VMEM((R, D_MODEL), BF16 if C % (2 * SUBLANE) == 0 else F32),
    ]
    return pl.pallas_call(
        functools.partial(_mixer_kernel, cfg),
        grid=(nb, nl),
        in_specs=in_specs,
        out_specs=out_specs,
        out_shape=out_shape,
        scratch_shapes=scratch,
        input_output_aliases=aliases,
        compiler_params=pltpu.CompilerParams(
            dimension_semantics=("arbitrary", "arbitrary"), vmem_limit_bytes=VMEM_LIMIT_BYTES),
        name="mixer",
    )(*operands)


FfnCfg = collections.namedtuple("FfnCfg", "T bb nl final zero_init n_alias")


def _ffn_kernel(cfg, *refs):
    T, bb, nl = cfg.T, cfg.bb, cfg.nl
    refs = list(refs)
    x_ref = refs.pop(0)
    sffn_ref = None if cfg.zero_init else refs.pop(0)
    n2_ref, wa_ref, wg_ref, fcw_ref, fcb_ref, wo_ref, fn_ref = refs[:7]
    del refs[:7]
    del refs[:cfg.n_alias]
    xo_ref, offn_ref, abuf, act = refs

    l = pl.program_id(1)
    R = bb * T
    x = x_ref[...]
    xn = (_rms_scale(x) * n2_ref[...]).astype(BF16)
    lo = FFNC_BASE - (FFN_CONV - 1)

    def init_hist():
        if cfg.zero_init:
            abuf[:, lo:FFNC_BASE, :] = jnp.zeros((bb, FFN_CONV - 1, D_FF), F32)
        else:
            abuf[:, lo:FFNC_BASE, :] = sffn_ref[...]

    if nl == 1:
        init_hist()
    else:
        @pl.when(l == 0)
        def _():
            init_hist()

        @pl.when(l > 0)
        def _():
            abuf[:, lo:FFNC_BASE, :] = offn_ref[...]
    f0 = 0
    for fw in FFN_SPLITS:
        a = _dot(xn, wa_ref[:, f0:f0 + fw])
        g = _dot(xn, wg_ref[:, f0:f0 + fw])
        abuf[:, FFNC_BASE:FFNC_BASE + T, f0:f0 + fw] = a.reshape(bb, T, fw)
        ac = fcb_ref[:, f0:f0 + fw][None]
        for kk in range(FFN_CONV):
            ac = ac + fcw_ref[kk:kk + 1, f0:f0 + fw][None] * abuf[:, lo + kk:lo + kk + T, f0:f0 + fw]
        offn_ref[:, :, f0:f0 + fw] = abuf[:, T + lo:T + FFNC_BASE, f0:f0 + fw]
        act[:, f0:f0 + fw] = (_silu(ac).reshape(R, fw) * g).astype(BF16)
        f0 += fw
    y = x + _dot(act[...], wo_ref[...])
    if cfg.final:
        y = _rms_scale(y) * fn_ref[...]
    xo_ref[...] = y


def _ffn_call(x2d, state_in, state_prev, wts, layer, Bn, L, bb, T, final):
    assert L % T == 0 and Bn % bb == 0
    nl = L // T
    assert bb == 1 or nl == 1
    nb = Bn // bb
    R = bb * T
    n2, w_ffn_in, fcw, fcb, w_ffn_out, fn = wts
    x_spec = pl.BlockSpec((R, D_MODEL), lambda i, l: (i * nl + l, 0))
    st_shape = jax.ShapeDtypeStruct((DEPTH, Bn, FFN_CONV - 1, D_FF), F32)
    operands = [x2d]
    in_specs = [x_spec]
    if state_in is not None:
        operands.append(state_in)
        in_specs.append(_state_spec(state_in, layer, bb))
    operands += [n2, w_ffn_in, w_ffn_in, fcw, fcb, w_ffn_out, fn]
    in_specs += [_layer_spec(n2, layer),
                 _layer_spec(w_ffn_in, layer, block=(D_MODEL, D_FF), col=0),
                 _layer_spec(w_ffn_in, layer, block=(D_MODEL, D_FF), col=1),
                 _layer_spec(fcw, layer), _layer_spec(fcb, layer), _layer_spec(w_ffn_out, layer), _const_spec(fn)]
    aliases = {}
    if state_prev is not None:
        aliases[len(operands)] = 1
        operands.append(state_prev)
        in_specs.append(pl.BlockSpec(memory_space=pl.ANY))
    cfg = FfnCfg(T, bb, nl, final, state_in is None, len(aliases))
    return pl.pallas_call(
        functools.partial(_ffn_kernel, cfg),
        grid=(nb, nl),
        in_specs=in_specs,
        out_specs=[x_spec, _state_spec(st_shape, layer, bb)],
        out_shape=[jax.ShapeDtypeStruct((Bn * L, D_MODEL), F32), st_shape],
        scratch_shapes=[pltpu.VMEM((bb, FFNC_BASE + T, D_FF), F32),
                        pltpu.VMEM((R, D_FF), BF16)],
        input_output_aliases=aliases,
        compiler_params=pltpu.CompilerParams(
            dimension_semantics=("arbitrary", "arbitrary"), vmem_limit_bytes=VMEM_LIMIT_BYTES),
        name="ffn",
    )(*operands)


def _prep_weights(norm1, w_in, w_gk, b_gk, gla_norm, conf_w, conf_b, conf_ln_w, conf_ln_b,
                  ssm_conv_w, ssm_conv_b, dt_bias, a_log, d_skip, ssm_norm, w_out,
                  norm2, w_ffn_in, ffn_conv_w, ffn_conv_b, w_ffn_out, final_norm):
    offs = np.concatenate([[0], np.cumsum(IN_SIZES)])
    wb = w_in.astype(BF16)
    seg = lambda i: wb[:, :, int(offs[i]):int(offs[i + 1])]
    zpad = lambda n: jnp.zeros((DEPTH, D_MODEL, n), BF16)
    w_in_p = jnp.concatenate([
        seg(0), zpad(QK_PAD - GLA_QK), seg(1), zpad(QK_PAD - GLA_QK), seg(2), seg(3),
        seg(5), seg(6), seg(7), seg(8), jnp.repeat(seg(9), SSM_HEADDIM, axis=2),
        seg(4), zpad(LANE - GLA_RANK)], axis=2)
    assert w_in_p.shape == (DEPTH, D_MODEL, NP)
    row = lambda v: v.reshape(DEPTH, 1, -1).astype(F32)
    wgk_p = jnp.zeros((DEPTH, LANE, QK_PAD), F32).at[:, :GLA_RANK, :GLA_QK].set(w_gk).astype(BF16)
    bgk_p = jnp.zeros((DEPTH, 1, QK_PAD), F32).at[:, 0, :GLA_QK].set(b_gk)
    rep64 = lambda v: row(jnp.repeat(v, SSM_HEADDIM, axis=1))
    mixer_w = (row(norm1), w_in_p, wgk_p, bgk_p, row(jnp.tile(gla_norm, (1, GLA_HEADS))),
               conf_w.astype(F32), row(conf_b), row(conf_ln_w), row(conf_ln_b),
               ssm_conv_w.astype(F32), row(ssm_conv_b), rep64(dt_bias), rep64(a_log),
               rep64(d_skip), row(ssm_norm), w_out.astype(BF16))
    ffn_w = (row(norm2), w_ffn_in.astype(BF16), ffn_conv_w.astype(F32), row(ffn_conv_b),
             w_ffn_out.astype(BF16), final_norm.reshape(1, -1).astype(F32))
    return mixer_w, ffn_w


def _run_group(x, states, mixer_w, ffn_w, bb, T, C, ffn_bb=None, ffn_T=None):
    ffn_bb = bb if ffn_bb is None else ffn_bb
    ffn_T = T if ffn_T is None else ffn_T
    Bn, L, _ = x.shape
    x2d = x.reshape(Bn * L, D_MODEL)
    out_states = tuple(jnp.zeros((DEPTH, Bn) + s, F32) for s in (
        (GLA_HEADS, GLA_DK, GLA_DV), (CONF_K - 1, CONF_CH), (SSM_CONV - 1, SSM_XBC),
        (SSM_HEADS, SSM_HEADDIM, SSM_STATE), (FFN_CONV - 1, D_FF)))
    for layer in range(DEPTH):
        prev4 = None if out_states is None else out_states[:4]
        prev_ffn = None if out_states is None else out_states[4]
        x2d, ngla, nconf, nsc, nssm = _mixer_call(
            x2d, None if states is None else states[:4], prev4, mixer_w, layer, Bn, L, bb, T, C)
        x2d, nffn = _ffn_call(x2d, None if states is None else states[4], prev_ffn, ffn_w, layer, Bn, L,
                              ffn_bb, ffn_T, final=(layer == DEPTH - 1))
        out_states = (ngla, nconf, nsc, nssm, nffn)
    return x2d.reshape(Bn, L, D_MODEL), out_states


FFN_ROWS = 512


def _tiles(Lp, Bs, Ls):
    tp = min(512, Lp)
    return tp, math.gcd(tp, GLA_CHUNK), min(16, Bs), math.gcd(Ls, GLA_CHUNK), min(max(FFN_ROWS // Ls, 1), Bs)


def kernel(x_prompt, x_sample, state_gla, state_conf_conv, state_ssm_conv, state_ssm, state_ffn_conv, norm1, w_in, w_gk, b_gk, gla_norm, conf_w, conf_b, conf_ln_w, conf_ln_b, ssm_conv_w, ssm_conv_b, dt_bias, a_log, d_skip, ssm_norm, w_out, norm2, w_ffn_in, ffn_conv_w, ffn_conv_b, w_ffn_out, final_norm):
    mixer_w, ffn_w = _prep_weights(norm1, w_in, w_gk, b_gk, gla_norm, conf_w, conf_b, conf_ln_w, conf_ln_b,
                                   ssm_conv_w, ssm_conv_b, dt_bias, a_log, d_skip, ssm_norm, w_out,
                                   norm2, w_ffn_in, ffn_conv_w, ffn_conv_b, w_ffn_out, final_norm)
    tp, cp, bbs, cs, bbf = _tiles(x_prompt.shape[1], x_sample.shape[0], x_sample.shape[1])
    y_p, st_p = _run_group(x_prompt, None, mixer_w, ffn_w, 1, tp, cp, ffn_T=min(2 * FFN_ROWS, x_prompt.shape[1]))
    s_states = (state_gla, state_conf_conv, state_ssm_conv, state_ssm, state_ffn_conv)
    y_s, st_s = _run_group(x_sample, s_states, mixer_w, ffn_w, bbs, x_sample.shape[1], cs, ffn_bb=bbf)
    return (y_p, y_s) + tuple(st_p) + tuple(st_s)
```
